```python
import math
import jax, jax.numpy as jnp
from jax import lax
import numpy as np

D_MODEL = 1024
BATCH = 8
SEQ = 8192
DEPTH = 2

CTX_LEN = 256
GRID_W = 64
CHUNK = 128
ROPE_BASE = 10000.0
f32 = jnp.float32

MIX_WIDTH = 2 * D_MODEL
SSD_INNER = D_MODEL
SSD_HEAD_DIM = 64
SSD_HEADS = SSD_INNER // SSD_HEAD_DIM
SSD_GROUPS = 2
SSD_HPG = SSD_HEADS // SSD_GROUPS
SSD_STATE = 128
SSD_CONV = 5
SSD_XBC = SSD_INNER + 2 * SSD_GROUPS * SSD_STATE
SSD_COLS = SSD_INNER + SSD_XBC + 2 * SSD_HEADS
DIFF_WIDTH = D_MODEL // 2
DIFF_V_DIM = 128
DIFF_HEADS = DIFF_WIDTH // DIFF_V_DIM
DIFF_HEAD_DIM = DIFF_V_DIM // 2
DIFF_QK = DIFF_HEADS * 2 * DIFF_HEAD_DIM
DIFF_COLS = 2 * DIFF_QK + DIFF_WIDTH
RET_WIDTH = D_MODEL // 2
RET_V_DIM = 128
RET_HEADS = RET_WIDTH // RET_V_DIM
RET_QK_DIM = RET_V_DIM // 2
RET_QK = RET_HEADS * RET_QK_DIM
RET_COLS = 2 * RET_QK + 2 * RET_WIDTH
IN_COLS = SSD_COLS + DIFF_COLS + RET_COLS
D_FF = 11 * D_MODEL // 4
FFN_CONV = 3
ALPHA = (2.0 * DEPTH) ** 0.25
BETA = (8.0 * DEPTH) ** -0.25

kernel_name = "hybrid_ssd_diffattn_retention_dit_block"


def _layernorm(x, w, b, eps=1e-5):
    xf = x.astype(f32)
    mu = xf.mean(-1, keepdims=True)
    var = jnp.mean(jnp.square(xf - mu), -1, keepdims=True)
    return ((xf - mu) * lax.rsqrt(var + eps) * w + b).astype(x.dtype)


def _groupnorm(x, w, eps=1e-5):
    xf = x.astype(f32)
    mu = xf.mean(-1, keepdims=True)
    var = jnp.mean(jnp.square(xf - mu), -1, keepdims=True)
    return ((xf - mu) * lax.rsqrt(var + eps) * w).astype(x.dtype)


def _rmsnorm(x, w, eps=1e-5):
    xf = x.astype(f32)
    return (xf * lax.rsqrt(jnp.mean(jnp.square(xf), -1, keepdims=True) + eps) * w).astype(x.dtype)


def _dwconv(x, w, b):
    k = w.shape[0]
    p = k // 2
    t = x.shape[1]
    xp = jnp.pad(x, ((0, 0), (p, p), (0, 0)))
    out = xp[:, 0:t] * w[0]
    for j in range(1, k):
        out = out + xp[:, j:j + t] * w[j]
    return out + b


def _rotate_half(x):
    h = x.shape[-1] // 2
    return jnp.concatenate([-x[..., h:], x[..., :h]], axis=-1)


def _rope(x, ang):
    ang2 = jnp.concatenate([ang, ang], axis=-1)
    return x * jnp.cos(ang2).astype(x.dtype) + _rotate_half(x) * jnp.sin(ang2).astype(x.dtype)


def _axial_rope(x, ang_row, ang_col):
    h = x.shape[-1] // 2
    return jnp.concatenate([_rope(x[..., :h], ang_row), _rope(x[..., h:], ang_col)], axis=-1)


def _chunked_scan(q, k, v, log_a, h0):
    bsz, t, g, n = q.shape
    r, p = v.shape[3], v.shape[4]
    nc = t // CHUNK
    qc = q.astype(f32).reshape(bsz, nc, CHUNK, g, n)
    kc = k.astype(f32).reshape(bsz, nc, CHUNK, g, n)
    vc = v.astype(f32).reshape(bsz, nc, CHUNK, g, r, p)
    acum = jnp.cumsum(log_a.astype(f32).reshape(bsz, nc, CHUNK, g, r).transpose(0, 1, 3, 4, 2), axis=-1)
    tril = jnp.tril(jnp.ones((CHUNK, CHUNK), dtype=bool))
    seg = acum[..., :, None] - acum[..., None, :]
    decay = jnp.exp(jnp.where(tril, seg, -jnp.inf))
    scores = jnp.einsum('bclgn,bcsgn->bcgls', qc, kc)
    y_diag = jnp.einsum('bcgrls,bcsgrp->bclgrp', scores[:, :, :, None] * decay, vc)
    to_end = jnp.exp(acum[..., -1:] - acum)
    states = jnp.einsum('bcsgn,bcgrs,bcsgrp->bcgrpn', kc, to_end, vc)
    chunk_decay = jnp.exp(acum[..., -1])

    def step(h, inp):
        st, dec = inp
        return dec[..., None, None] * h + st, h

    h_last, h_start = lax.scan(step, h0.astype(f32),
                               (jnp.moveaxis(states, 1, 0), jnp.moveaxis(chunk_decay, 1, 0)))
    h_start = jnp.moveaxis(h_start, 0, 1)
    y_off = jnp.einsum('bclgn,bcgrpn,bcgrl->bclgrp', qc, h_start, jnp.exp(acum))
    y = (y_diag + y_off).reshape(bsz, t, g, r, p).astype(v.dtype)
    return y, h_last


def _bidir_scan(q, k, v_f, la_f, v_b, la_b, h0_f, h0_b):
    y_f, h_f = _chunked_scan(q, k, v_f, la_f, h0_f)
    flip = lambda a: jnp.flip(a, axis=1)
    y_b, h_b = _chunked_scan(flip(q), flip(k), flip(v_b), flip(la_b), h0_b)
    return y_f + flip(y_b), h_f, h_b


def _ssd_mixer(p_ssd, conv_w, conv_b, a_log, dt_bias, d_skip, norm_w, h0_f, h0_b):
    bsz, t, _ = p_ssd.shape
    z, xbc, dt = jnp.split(p_ssd, [SSD_INNER, SSD_INNER + SSD_XBC], axis=-1)
    xbc = jax.nn.silu(_dwconv(xbc, conv_w, conv_b))
    xs, bm, cm = jnp.split(xbc, [SSD_INNER, SSD_INNER + SSD_GROUPS * SSD_STATE], axis=-1)
    xs = xs.reshape(bsz, t, SSD_GROUPS, SSD_HPG, SSD_HEAD_DIM)
    bm = bm.reshape(bsz, t, SSD_GROUPS, SSD_STATE)
    cm = cm.reshape(bsz, t, SSD_GROUPS, SSD_STATE)
    dt = jax.nn.softplus(dt.astype(f32).reshape(bsz, t, 2, SSD_HEADS) + dt_bias.astype(f32))
    a = -jnp.exp(a_log.astype(f32))
    la = (dt * a).reshape(bsz, t, 2, SSD_GROUPS, SSD_HPG)
    dt = dt.reshape(bsz, t, 2, SSD_GROUPS, SSD_HPG)
    v_f = xs * dt[:, :, 0][..., None]
    v_b = xs * dt[:, :, 1][..., None]
    y, h_f, h_b = _bidir_scan(cm, bm, v_f, la[:, :, 0], v_b, la[:, :, 1], h0_f, h0_b)
    y = y + xs * d_skip.reshape(SSD_GROUPS, SSD_HPG, 1)
    y = y.reshape(bsz, t, SSD_INNER)
    y = _rmsnorm(y * jax.nn.silu(z), norm_w)
    return y.astype(p_ssd.dtype), h_f, h_b


def _ret_mixer(p_ret, ang, decay_raw, norm_w, h0_f, h0_b):
    bsz, t, _ = p_ret.shape
    q, k, v, g = jnp.split(p_ret, [RET_QK, 2 * RET_QK, 2 * RET_QK + RET_WIDTH], axis=-1)
    q = _rope(q.reshape(bsz, t, RET_HEADS, RET_QK_DIM), ang)
    k = _rope(k.reshape(bsz, t, RET_HEADS, RET_QK_DIM), ang) * (RET_QK_DIM ** -0.5)
    v = v.reshape(bsz, t, RET_HEADS, 1, RET_V_DIM)
    log_gamma = -jnp.exp(decay_raw.astype(f32))
    la_f = jnp.broadcast_to(log_gamma[0][:, None], (bsz, t, RET_HEADS, 1))
    la_b = jnp.broadcast_to(log_gamma[1][:, None], (bsz, t, RET_HEADS, 1))
    y, h_f, h_b = _bidir_scan(q, k, v, la_f, v, la_b, h0_f, h0_b)
    y = _groupnorm(y.reshape(bsz, t, RET_HEADS, RET_V_DIM), norm_w).reshape(bsz, t, RET_WIDTH)
    return (y * jax.nn.silu(g)).astype(p_ret.dtype), h_f, h_b


def _diff_split(p_diff, ang_row, ang_col):
    bsz, t, _ = p_diff.shape
    q, k, v = jnp.split(p_diff, [DIFF_QK, 2 * DIFF_QK], axis=-1)
    q = q.reshape(bsz, t, DIFF_HEADS, 2, DIFF_HEAD_DIM)
    k = k.reshape(bsz, t, DIFF_HEADS, 2, DIFF_HEAD_DIM)
    if ang_row is not None:
        q = _axial_rope(q, ang_row, ang_col)
        k = _axial_rope(k, ang_row, ang_col)
    q = q.transpose(0, 2, 3, 1, 4)
    k = k.transpose(0, 2, 3, 1, 4)
    v = v.reshape(bsz, t, DIFF_HEADS, DIFF_V_DIM).transpose(0, 2, 1, 3)
    return q, k, v


def _diff_attend(q, k, v, lam):
    s = jnp.einsum('bhmqd,bhmkd->bhmqk', q, k).astype(f32) * (DIFF_HEAD_DIM ** -0.5)
    p = jax.nn.softmax(s, axis=-1)
    a = p[:, :, 0] - lam * p[:, :, 1]
    return jnp.einsum('bhqk,bhkv->bhqv', a.astype(v.dtype), v)


def _diff_out(o, norm_w, lam_init):
    bsz, h, t, vd = o.shape
    o = _rmsnorm(o, norm_w) * (1.0 - lam_init)
    return o.transpose(0, 2, 1, 3).reshape(bsz, t, h * vd)


def _conv_ffn(h, w_up, conv_w, conv_b, w_down):
    u, v = jnp.split(h @ w_up, 2, axis=-1)
    return (jax.nn.gelu(_dwconv(u, conv_w, conv_b), approximate=False) * v) @ w_down


def setup_inputs(seed: int = 0) -> dict:
    key = jax.random.key(seed)
    ks = jax.random.split(key, 32)
    L, D = DEPTH, D_MODEL
    nrm = lambda k, shape, scale: jax.random.normal(k, shape, f32) * scale
    dt0 = jnp.exp(jax.random.uniform(ks[10], (L, 2, SSD_HEADS), f32, math.log(1e-3), math.log(1e-1)))
    ret_init = jnp.log(-jnp.log1p(-(2.0 ** (-5.0 - jnp.arange(RET_HEADS, dtype=f32)))))
    return {
        "x": nrm(ks[0], (BATCH, SEQ, D), 1.0),
        "c": nrm(ks[1], (BATCH, D), 1.0),
        "ctx": nrm(ks[2], (BATCH, CTX_LEN, D), 1.0),
        "c_ctx": nrm(ks[3], (D,), 1.0),
        "w_ada": nrm(ks[4], (L, D, 6 * D), 0.5 * D ** -0.5),
        "b_ada": nrm(ks[5], (L, 6 * D), 0.02),
        "w_in": nrm(ks[6], (L, D, IN_COLS), D ** -0.5),
        "ssd_conv_w": nrm(ks[7], (L, SSD_CONV, SSD_XBC), SSD_CONV ** -0.5),
        "ssd_conv_b": nrm(ks[8], (L, SSD_XBC), 0.02),
        "ssd_a_log": jnp.log(jax.random.uniform(ks[9], (L, 2, SSD_HEADS), f32, 1.0, 16.0)),
        "ssd_dt_bias": dt0 + jnp.log(-jnp.expm1(-dt0)),
        "ssd_d": 1.0 + nrm(ks[11], (L, SSD_HEADS), 0.1),
        "ssd_norm_w": 1.0 + nrm(ks[12], (L, SSD_INNER), 0.05),
        "diff_lambda": nrm(ks[13], (L, 4, DIFF_HEAD_DIM), 0.1),
        "diff_norm_w": 1.0 + nrm(ks[14], (L, DIFF_V_DIM), 0.05),
        "ret_decay": ret_init + nrm(ks[15], (L, 2, RET_HEADS), 0.05),
        "ret_norm_w": 1.0 + nrm(ks[16], (L, RET_V_DIM), 0.05),
        "w_out": nrm(ks[17], (L, MIX_WIDTH, D), BETA * MIX_WIDTH ** -0.5),
        "ln1_w": 1.0 + nrm(ks[18], (L, D), 0.05),
        "ln1_b": nrm(ks[19], (L, D), 0.02),
        "ffn_w_up": nrm(ks[20], (L, D, 2 * D_FF), D ** -0.5),
        "ffn_conv_w": nrm(ks[21], (L, FFN_CONV, D_FF), FFN_CONV ** -0.5),
        "ffn_conv_b": nrm(ks[22], (L, D_FF), 0.02),
        "ffn_w_down": nrm(ks[23], (L, D_FF, D), BETA * D_FF ** -0.5),
        "ln2_w": 1.0 + nrm(ks[24], (L, D), 0.05),
        "ln2_b": nrm(ks[25], (L, D), 0.02),
    }


def reference(x, c, ctx, c_ctx, w_ada, b_ada, w_in, ssd_conv_w, ssd_conv_b, ssd_a_log, ssd_dt_bias, ssd_d,
              ssd_norm_w, diff_lambda, diff_norm_w, ret_decay, ret_norm_w, w_out, ln1_w, ln1_b, ffn_w_up,
              ffn_conv_w, ffn_conv_b, ffn_w_down, ln2_w, ln2_b):
    bsz, s, _ = x.shape
    n_ctx = ctx.shape[1]
    rows = s // GRID_W
    row = jnp.repeat(jnp.arange(rows, dtype=f32), GRID_W)
    col = jnp.tile(jnp.arange(GRID_W, dtype=f32), rows)
    n_ax = DIFF_HEAD_DIM // 4
    inv_ax = 1.0 / (ROPE_BASE ** (jnp.arange(n_ax, dtype=f32) / n_ax))
    ang_row = (row[:, None] * inv_ax)[:, None, None, :]
    ang_col = (col[:, None] * inv_ax)[:, None, None, :]
    inv_ret = 1.0 / (ROPE_BASE ** jnp.linspace(0.0, 1.0, RET_QK_DIM // 2, dtype=f32))
    ang_ret_c = (jnp.arange(n_ctx, dtype=f32)[:, None] * inv_ret)[:, None, :]
    ang_ret_l = ((n_ctx + jnp.arange(s, dtype=f32))[:, None] * inv_ret)[:, None, :]
    silu_c = jax.nn.silu(c)
    silu_cc = jax.nn.silu(c_ctx)
    nb = s // CHUNK
    xc = ctx
    for li in range(DEPTH):
        last = li == DEPTH - 1
        sh_a, sc_a, g_a, sh_f, sc_f, g_f = jnp.split((silu_c @ w_ada[li] + b_ada[li])[:, None, :], 6, axis=-1)
        csh_a, csc_a, cg_a, csh_f, csc_f, cg_f = jnp.split((silu_cc @ w_ada[li] + b_ada[li])[None, None, :], 6, axis=-1)

        proj = (x * (1.0 + sc_a) + sh_a) @ w_in[li]
        proj_c = (xc * (1.0 + csc_a) + csh_a) @ w_in[li]
        p_ssd, p_diff, p_ret = jnp.split(proj, [SSD_COLS, SSD_COLS + DIFF_COLS], axis=-1)
        pc_ssd, pc_diff, pc_ret = jnp.split(proj_c, [SSD_COLS, SSD_COLS + DIFF_COLS], axis=-1)

        zero_ssd = jnp.zeros((bsz, SSD_GROUPS, SSD_HPG, SSD_HEAD_DIM, SSD_STATE), f32)
        ssd_args = (ssd_conv_w[li], ssd_conv_b[li], ssd_a_log[li], ssd_dt_bias[li], ssd_d[li], ssd_norm_w[li])
        yc_ssd, hs_f, hs_b = _ssd_mixer(pc_ssd, *ssd_args, zero_ssd, zero_ssd)
        y_ssd, _, _ = _ssd_mixer(p_ssd, *ssd_args, hs_f, hs_b)

        zero_ret = jnp.zeros((bsz, RET_HEADS, 1, RET_V_DIM, RET_QK_DIM), f32)
        yc_ret, hr_f, hr_b = _ret_mixer(pc_ret, ang_ret_c, ret_decay[li], ret_norm_w[li], zero_ret, zero_ret)
        y_ret, _, _ = _ret_mixer(p_ret, ang_ret_l, ret_decay[li], ret_norm_w[li], hr_f, hr_b)

        lam_init = 0.8 - 0.6 * math.exp(-0.3 * li)
        lq1, lk1, lq2, lk2 = diff_lambda[li].astype(f32)
        lam = jnp.exp(jnp.sum(lq1 * lk1)) - jnp.exp(jnp.sum(lq2 * lk2)) + lam_init
        q_l, k_l, v_l = _diff_split(p_diff, ang_row, ang_col)
        q_c, k_c, v_c = _diff_split(pc_diff, None, None)
        k_all = jnp.concatenate([k_l, k_c], axis=3)
        v_all = jnp.concatenate([v_l, v_c], axis=2)
        q_blocks = jnp.moveaxis(q_l.reshape(bsz, DIFF_HEADS, 2, nb, CHUNK, DIFF_HEAD_DIM), 3, 0)
        o_l = lax.map(lambda qb: _diff_attend(qb, k_all, v_all, lam), q_blocks)
        o_l = jnp.moveaxis(o_l, 0, 2).reshape(bsz, DIFF_HEADS, s, DIFF_V_DIM)
        y_diff = _diff_out(o_l, diff_norm_w[li], lam_init)

        y = jnp.concatenate([y_ssd, y_diff, y_ret], axis=-1) @ w_out[li]
        x = _layernorm(ALPHA * x + g_a * y, ln1_w[li], ln1_b[li])
        if not last:
            yc_diff = _diff_out(_diff_attend(q_c, k_c, v_c, lam), diff_norm_w[li], lam_init)
            yc = jnp.concatenate([yc_ssd, yc_diff, yc_ret], axis=-1) @ w_out[li]
            xc = _layernorm(ALPHA * xc + cg_a * yc, ln1_w[li], ln1_b[li])

        f = _conv_ffn(x * (1.0 + sc_f) + sh_f, ffn_w_up[li], ffn_conv_w[li], ffn_conv_b[li], ffn_w_down[li])
        x = _layernorm(ALPHA * x + g_f * f, ln2_w[li], ln2_b[li])
        if not last:
            fc = _conv_ffn(xc * (1.0 + csc_f) + csh_f, ffn_w_up[li], ffn_conv_w[li], ffn_conv_b[li], ffn_w_down[li])
            xc = _layernorm(ALPHA * xc + cg_f * fc, ln2_w[li], ln2_b[li])
    return x
```

```python
import functools
import math

import jax
import jax.numpy as jnp
from jax import lax
from jax.experimental import pallas as pl
from jax.experimental.pallas import tpu as pltpu

f32 = jnp.float32
bf16 = jnp.bfloat16
HIGHEST = lax.Precision.HIGHEST

D_MODEL = 1024
GRID_W = 64
CHUNK = 128
ROPE_BASE = 10000.0
SSD_INNER = D_MODEL
SSD_HEAD_DIM = 64
SSD_HEADS = SSD_INNER // SSD_HEAD_DIM
SSD_GROUPS = 2
SSD_HPG = SSD_HEADS // SSD_GROUPS
SSD_STATE = 128
SSD_CONV = 5
SSD_XBC = SSD_INNER + 2 * SSD_GROUPS * SSD_STATE
SSD_COLS = SSD_INNER + SSD_XBC + 2 * SSD_HEADS
DIFF_WIDTH = D_MODEL // 2
DIFF_V_DIM = 128
DIFF_HEADS = DIFF_WIDTH // DIFF_V_DIM
DIFF_HEAD_DIM = DIFF_V_DIM // 2
DIFF_QK = DIFF_HEADS * 2 * DIFF_HEAD_DIM
DIFF_COLS = 2 * DIFF_QK + DIFF_WIDTH
RET_WIDTH = D_MODEL // 2
RET_V_DIM = 128
RET_HEADS = RET_WIDTH // RET_V_DIM
RET_QK_DIM = RET_V_DIM // 2
RET_QK = RET_HEADS * RET_QK_DIM
RET_COLS = 2 * RET_QK + 2 * RET_WIDTH
MIX_WIDTH = 2 * D_MODEL
D_FF = 11 * D_MODEL // 4
FFN_CONV = 3
EPS = 1e-5

LANES_V7X = 128
SUBLANES_V7X = 8
VMEM_LIMIT_V7X = 56 * 1024 * 1024

COL_Z = 0
COL_XBC = 8
COL_DQ = 20
COL_DQR = 24
COL_DK = 28
COL_DKR = 32
COL_DV = 36
COL_RQ = 40
COL_RQR = 42
COL_RK = 44
COL_RKR = 46
COL_RV = 48
COL_RG = 52
COL_DT = 56
N_EXT = 58 * LANES_V7X
NEG_BIG = -1e30


def _cparams(sem):
    return pltpu.CompilerParams(dimension_semantics=sem, vmem_limit_bytes=VMEM_LIMIT_V7X)


def _silu(x):
    return x * (1.0 / (1.0 + jnp.exp(-x)))


def _layernorm(x, w, b):
    mu = jnp.mean(x, axis=-1, keepdims=True)
    xc = x - mu
    var = jnp.mean(xc * xc, axis=-1, keepdims=True)
    return xc * lax.rsqrt(var + EPS) * w + b


def _split2(x):
    hi = x.astype(bf16)
    lo = (x - hi.astype(f32)).astype(bf16)
    return hi, lo


def _ada_kernel(c_ref, w_ref, b_ref, o_ref):
    c = c_ref[...]
    o_ref[...] = jnp.dot(_silu(c), w_ref[...], precision=HIGHEST, preferred_element_type=f32) + b_ref[...]


def _ada_call(cvecs, w_ada, b_ada):
    depth, d, n = w_ada.shape
    r = cvecs.shape[0]
    tn = 1536
    return pl.pallas_call(
        _ada_kernel,
        grid=(depth, n // tn),
        in_specs=[
            pl.BlockSpec((r, d), lambda l, j: (0, 0)),
            pl.BlockSpec((None, d, tn), lambda l, j: (l, 0, j)),
            pl.BlockSpec((None, 1, tn), lambda l, j: (l, 0, j)),
        ],
        out_specs=pl.BlockSpec((None, r, tn), lambda l, j: (l, 0, j)),
        out_shape=jax.ShapeDtypeStruct((depth, r, n), f32),
        compiler_params=_cparams(("parallel", "parallel")),
        name="ada",
    )(cvecs, w_ada, b_ada.reshape(depth, 1, n))


def _modmm_kernel(x_ref, sc_ref, sh_ref, w_ref, o_ref):
    xm = x_ref[...] * (1.0 + sc_ref[...]) + sh_ref[...]
    o_ref[...] = jnp.dot(xm.astype(bf16), w_ref[...], preferred_element_type=f32).astype(o_ref.dtype)


def _modmm_call(x, sc, sh, w_ext):
    bsz, t, d = x.shape
    n = w_ext.shape[1]
    tm = min(512, t)
    tn = n // 2
    return pl.pallas_call(
        _modmm_kernel,
        grid=(n // tn, bsz, t // tm),
        in_specs=[
            pl.BlockSpec((None, tm, d), lambda j, b, i: (b, i, 0)),
            pl.BlockSpec((None, 1, d), lambda j, b, i: (b, 0, 0)),
            pl.BlockSpec((None, 1, d), lambda j, b, i: (b, 0, 0)),
            pl.BlockSpec((d, tn), lambda j, b, i: (0, j)),
        ],
        out_specs=pl.BlockSpec((None, tm, tn), lambda j, b, i: (b, i, j)),
        out_shape=jax.ShapeDtypeStruct((bsz, t, n), f32),
        compiler_params=_cparams(("parallel", "parallel", "parallel")),
        name="in_proj",
    )(x, sc, sh, w_ext)


def _ssd_conv_kernel(cur_ref, prev_ref, next_ref, w_ref, b_ref, o_ref, ext_ref, *, tm, nt):
    i = pl.program_id(1)
    h = SUBLANES_V7X
    ext_ref[0:h, :] = jnp.where(i > 0, prev_ref[...], 0.0)
    ext_ref[h:h + tm, :] = cur_ref[...]
    ext_ref[h + tm:2 * h + tm, :] = jnp.where(i < nt - 1, next_ref[...], 0.0)
    p = SSD_CONV // 2
    acc = ext_ref[h - p:h - p + tm, :] * w_ref[0:1, :]
    for j in range(1, SSD_CONV):
        acc = acc + ext_ref[h - p + j:h - p + j + tm, :] * w_ref[j:j + 1, :]
    o_ref[...] = _silu(acc + b_ref[...])


def _ssd_conv_call(proj, conv_w, conv_b):
    bsz, t, _ = proj.shape
    tm = min(512, t)
    nt = t // tm
    cw = 512
    nc = SSD_XBC // cw
    c0 = COL_XBC * LANES_V7X // cw
    hb = tm // SUBLANES_V7X
    last_hb = t // SUBLANES_V7X - 1
    return pl.pallas_call(
        functools.partial(_ssd_conv_kernel, tm=tm, nt=nt),
        grid=(bsz, nt, nc),
        in_specs=[
            pl.BlockSpec((None, tm, cw), lambda b, i, c: (b, i, c0 + c)),
            pl.BlockSpec((None, SUBLANES_V7X, cw), lambda b, i, c: (b, jnp.maximum(i * hb - 1, 0), c0 + c)),
            pl.BlockSpec((None, SUBLANES_V7X, cw), lambda b, i, c: (b, jnp.minimum((i + 1) * hb, last_hb), c0 + c)),
            pl.BlockSpec((SSD_CONV, cw), lambda b, i, c: (0, c)),
            pl.BlockSpec((1, cw), lambda b, i, c: (0, c)),
        ],
        out_specs=pl.BlockSpec((None, tm, cw), lambda b, i, c: (b, i, c)),
        out_shape=jax.ShapeDtypeStruct((bsz, t, SSD_XBC), f32),
        scratch_shapes=[pltpu.VMEM((tm + 2 * SUBLANES_V7X, cw), f32)],
        compiler_params=_cparams(("parallel", "parallel", "parallel")),
        name="ssd_conv",
    )(proj, proj, proj, conv_w, conv_b.reshape(1, SSD_XBC))


def _ssd_scan_kernel(cq_ref, bk_ref, x_ref, dt_ref, alog_ref, dtb_ref, tri_ref, e_ref, h0_ref,
                     y_ref, hout_ref, h_ref, *, nc):
    j = pl.program_id(2)
    L = CHUNK
    gw = SSD_HPG * SSD_HEAD_DIM

    @pl.when(j == 0)
    def _():
        h_ref[...] = h0_ref[...]

    tri = tri_ref[...]
    mask = tri > 0.5
    a_neg = -jnp.exp(alog_ref[...])
    sp = jax.nn.softplus(dt_ref[...] + dtb_ref[...])
    la = sp * a_neg
    cum = jnp.dot(tri, la, precision=HIGHEST, preferred_element_type=f32)
    cum_t = cum.T
    sp_t = sp.T
    tot = jnp.sum(la, axis=0, keepdims=True)
    w_end = jnp.exp(tot - cum) * sp
    e_cum = jnp.exp(cum)
    dec = jnp.broadcast_to(jnp.exp(tot), (SUBLANES_V7X, LANES_V7X))
    pieces = _split2(w_end) + _split2(e_cum) + _split2(dec)
    r = jnp.dot(jnp.concatenate(pieces, axis=0), e_ref[...], preferred_element_type=f32)
    w_x = r[0:L] + r[L:2 * L]
    ec_x = r[2 * L:3 * L] + r[3 * L:4 * L]
    dec_x = r[4 * L:4 * L + 1] + r[4 * L + SUBLANES_V7X:4 * L + SUBLANES_V7X + 1]

    x = x_ref[...]
    xw = (x * w_x).astype(bf16)
    lane = lax.broadcasted_iota(jnp.int32, (L, LANES_V7X), 1)
    lo_half = lane < SSD_HEAD_DIM

    for g in range(SSD_GROUPS):
        qg = cq_ref[:, g * SSD_STATE:(g + 1) * SSD_STATE]
        kg = bk_ref[:, g * SSD_STATE:(g + 1) * SSD_STATE]
        qg_b = qg.astype(bf16)
        scores = lax.dot_general(qg_b, kg.astype(bf16), (((1,), (1,)), ((), ())), preferred_element_type=f32)
        kg_t = kg.T.astype(bf16)
        h_old = h_ref[g]
        y_off = jnp.dot(qg_b, h_old.astype(bf16), preferred_element_type=f32) * ec_x[:, g * gw:(g + 1) * gw]
        h_ref[g] = dec_x[:, g * gw:(g + 1) * gw] * h_old + jnp.dot(
            kg_t, xw[:, g * gw:(g + 1) * gw], preferred_element_type=f32)
        outs = []
        for pp in range(SSD_HPG // 2):
            ms = []
            for hh in range(2):
                head = g * SSD_HPG + 2 * pp + hh
                seg = cum[:, head:head + 1] - cum_t[head:head + 1, :]
                decay = jnp.exp(jnp.where(mask, seg, NEG_BIG))
                ms.append((scores * decay * sp_t[head:head + 1, :]).astype(bf16))
            c_lo = g * gw + pp * LANES_V7X
            xp = x[:, c_lo:c_lo + LANES_V7X]
            rhs = jnp.concatenate([jnp.where(lo_half, xp, 0.0), jnp.where(lo_half, 0.0, xp)], axis=0).astype(bf16)
            outs.append(jnp.dot(jnp.concatenate(ms, axis=1), rhs, preferred_element_type=f32))
        y_ref[:, g * gw:(g + 1) * gw] = jnp.concatenate(outs, axis=1) + y_off

    @pl.when(j == nc - 1)
    def _():
        hout_ref[...] = h_ref[...]


def _ssd_scan_call(xbc, proj, a_log, dt_bias, h0):
    bsz, t, _ = xbc.shape
    nc = t // CHUNK
    L = CHUNK
    gw = SSD_HPG * SSD_HEAD_DIM

    def chunk(d, j):
        return j + d * (nc - 1 - 2 * j)

    alog = jnp.zeros((2, 1, LANES_V7X), f32).at[:, 0, :SSD_HEADS].set(a_log)
    dtb = jnp.zeros((2, 1, LANES_V7X), f32).at[:, 0, :SSD_HEADS].set(dt_bias)
    idx = jnp.arange(L)
    tri = jnp.stack([(idx[None, :] <= idx[:, None]), (idx[None, :] >= idx[:, None])]).astype(f32)
    rows = jnp.arange(LANES_V7X)[:, None]
    heads = (jnp.arange(SSD_INNER) // SSD_HEAD_DIM)[None, :]
    expand = (rows == heads).astype(bf16)
    bcol = SSD_INNER // (SSD_GROUPS * SSD_STATE)
    return pl.pallas_call(
        functools.partial(_ssd_scan_kernel, nc=nc),
        grid=(bsz, 2, nc),
        in_specs=[
            pl.BlockSpec((None, L, SSD_GROUPS * SSD_STATE), lambda b, d, j: (b, chunk(d, j), bcol + 1)),
            pl.BlockSpec((None, L, SSD_GROUPS * SSD_STATE), lambda b, d, j: (b, chunk(d, j), bcol)),
            pl.BlockSpec((None, L, SSD_INNER), lambda b, d, j: (b, chunk(d, j), 0)),
            pl.BlockSpec((None, L, LANES_V7X), lambda b, d, j: (b, chunk(d, j), COL_DT + d)),
            pl.BlockSpec((None, 1, LANES_V7X), lambda b, d, j: (d, 0, 0)),
            pl.BlockSpec((None, 1, LANES_V7X), lambda b, d, j: (d, 0, 0)),
            pl.BlockSpec((None, L, L), lambda b, d, j: (d, 0, 0)),
            pl.BlockSpec((LANES_V7X, SSD_INNER), lambda b, d, j: (0, 0)),
            pl.BlockSpec((None, None, SSD_GROUPS, SSD_STATE, gw), lambda b, d, j: (b, d, 0, 0, 0)),
        ],
        out_specs=[
            pl.BlockSpec((None, None, L, SSD_INNER), lambda b, d, j: (b, d, chunk(d, j), 0)),
            pl.BlockSpec((None, None, SSD_GROUPS, SSD_STATE, gw), lambda b, d, j: (b, d, 0, 0, 0)),
        ],
        out_shape=[
            jax.ShapeDtypeStruct((bsz, 2, t, SSD_INNER), f32),
            jax.ShapeDtypeStruct((bsz, 2, SSD_GROUPS, SSD_STATE, gw), f32),
        ],
        scratch_shapes=[pltpu.VMEM((SSD_GROUPS, SSD_STATE, gw), f32)],
        compiler_params=_cparams(("parallel", "parallel", "arbitrary")),
        name="ssd_scan",
    )(xbc, xbc, xbc, proj, alog, dtb, tri, expand, h0)


def _ret_scan_kernel(q_ref, qr_ref, k_ref, kr_ref, v_ref, cos_ref, sin_ref, rd_ref, h0_ref,
                     y_ref, hout_ref, h_ref, dec_ref, ecx_ref, tex_ref, dst_ref, *, nc):
    d = pl.program_id(1)
    j = pl.program_id(2)
    L = CHUNK
    npair = RET_HEADS // 2

    @pl.when(j == 0)
    def _():
        h_ref[...] = h0_ref[...]
        lg = -jnp.exp(rd_ref[...])
        li = lax.broadcasted_iota(jnp.int32, (L, L), 0)
        si = lax.broadcasted_iota(jnp.int32, (L, L), 1)
        dist = jnp.where(d == 0, li - si, si - li)
        causal = dist >= 0
        distf = dist.astype(f32)
        lane = lax.broadcasted_iota(jnp.int32, (1, RET_WIDTH), 1)
        lgx = jnp.zeros((1, RET_WIDTH), f32)
        for h in range(RET_HEADS):
            lgh = lg[:, h:h + 1]
            dec_ref[h] = jnp.exp(jnp.where(causal, distf * lgh, NEG_BIG))
            lgx = jnp.where((lane >= h * RET_V_DIM) & (lane < (h + 1) * RET_V_DIM), lgh, lgx)
        pos = lax.broadcasted_iota(jnp.int32, (L, 1), 0)
        steps = jnp.where(d == 0, pos + 1, L - pos).astype(f32)
        cumx = steps * lgx
        totx = float(L) * lgx
        ecx_ref[...] = jnp.exp(cumx)
        tex_ref[...] = jnp.exp(totx - cumx)
        rowi = lax.broadcasted_iota(jnp.int32, (2 * RET_QK_DIM, RET_V_DIM), 0)
        for pp in range(npair):
            d0 = jnp.exp(float(L) * lg[:, 2 * pp:2 * pp + 1])
            d1 = jnp.exp(float(L) * lg[:, 2 * pp + 1:2 * pp + 2])
            dst_ref[pp] = jnp.where(rowi < RET_QK_DIM, d0, d1)

    cos = cos_ref[...]
    sin = sin_ref[...]
    q = q_ref[...] * cos + qr_ref[...] * sin
    k = (k_ref[...] * cos + kr_ref[...] * sin) * (RET_QK_DIM ** -0.5)
    v = v_ref[...]
    vw = (v * tex_ref[...]).astype(bf16)
    vb = v.astype(bf16)
    ecx = ecx_ref[...]
    hi_half = lax.broadcasted_iota(jnp.int32, (L, LANES_V7X), 1) >= RET_QK_DIM
    zeros_v = jnp.zeros((L, RET_V_DIM), bf16)

    for pp in range(npair):
        qp = q[:, pp * LANES_V7X:(pp + 1) * LANES_V7X]
        kp = k[:, pp * LANES_V7X:(pp + 1) * LANES_V7X]
        kp_b = kp.astype(bf16)
        kp_t = kp.T.astype(bf16)
        r_old = h_ref[pp]
        r_b = r_old.astype(bf16)
        ms, offs = [], []
        for hh in range(2):
            h = 2 * pp + hh
            qm = (jnp.where(hi_half, qp, 0.0) if hh else jnp.where(hi_half, 0.0, qp)).astype(bf16)
            scores = lax.dot_general(qm, kp_b, (((1,), (1,)), ((), ())), preferred_element_type=f32)
            ms.append((scores * dec_ref[h]).astype(bf16))
            offs.append(jnp.dot(qm, r_b, preferred_element_type=f32))
        va = vb[:, (2 * pp) * RET_V_DIM:(2 * pp + 1) * RET_V_DIM]
        vc = vb[:, (2 * pp + 1) * RET_V_DIM:(2 * pp + 2) * RET_V_DIM]
        rhs = jnp.concatenate([jnp.concatenate([va, zeros_v], axis=1),
                               jnp.concatenate([zeros_v, vc], axis=1)], axis=0)
        y_diag = jnp.dot(jnp.concatenate(ms, axis=1), rhs, preferred_element_type=f32)
        c0 = 2 * pp * RET_V_DIM
        y_ref[:, c0:c0 + 2 * RET_V_DIM] = y_diag + jnp.concatenate(offs, axis=1) * ecx[:, c0:c0 + 2 * RET_V_DIM]
        upd = jnp.dot(kp_t, vw[:, c0:c0 + 2 * RET_V_DIM], preferred_element_type=f32)
        new = jnp.concatenate([upd[0:RET_QK_DIM, 0:RET_V_DIM],
                               upd[RET_QK_DIM:2 * RET_QK_DIM, RET_V_DIM:2 * RET_V_DIM]], axis=0)
        h_ref[pp] = dst_ref[pp] * r_old + new

    @pl.when(j == nc - 1)
    def _():
        hout_ref[...] = h_ref[...]


def _ret_scan_call(proj, cos, sin, decay_raw, h0):
    bsz, t, _ = proj.shape
    nc = t // CHUNK
    L = CHUNK
    npair = RET_HEADS // 2

    def chunk(d, j):
        return j + d * (nc - 1 - 2 * j)

    rd = jnp.zeros((2, 1, LANES_V7X), f32).at[:, 0, :RET_HEADS].set(decay_raw)
    qk = lambda c: pl.BlockSpec((None, L, RET_QK), lambda b, d, j: (b, chunk(d, j), c // 2))
    tab = pl.BlockSpec((L, RET_QK), lambda b, d, j: (chunk(d, j), 0))
    st = pl.BlockSpec((None, None, npair, 2 * RET_QK_DIM, RET_V_DIM), lambda b, d, j: (b, d, 0, 0, 0))
    return pl.pallas_call(
        functools.partial(_ret_scan_kernel, nc=nc),
        grid=(bsz, 2, nc),
        in_specs=[
            qk(COL_RQ), qk(COL_RQR), qk(COL_RK), qk(COL_RKR),
            pl.BlockSpec((None, L, RET_WIDTH), lambda b, d, j: (b, chunk(d, j), COL_RV // 4)),
            tab, tab,
            pl.BlockSpec((None, 1, LANES_V7X), lambda b, d, j: (d, 0, 0)),
            st,
        ],
        out_specs=[
            pl.BlockSpec((None, None, L, RET_WIDTH), lambda b, d, j: (b, d, chunk(d, j), 0)),
            st,
        ],
        out_shape=[
            jax.ShapeDtypeStruct((bsz, 2, t, RET_WIDTH), f32),
            jax.ShapeDtypeStruct((bsz, 2, npair, 2 * RET_QK_DIM, RET_V_DIM), f32),
        ],
        scratch_shapes=[
            pltpu.VMEM((npair, 2 * RET_QK_DIM, RET_V_DIM), f32),
            pltpu.VMEM((RET_HEADS, L, L), f32),
            pltpu.VMEM((L, RET_WIDTH), f32),
            pltpu.VMEM((L, RET_WIDTH), f32),
            pltpu.VMEM((npair, 2 * RET_QK_DIM, RET_V_DIM), f32),
        ],
        compiler_params=_cparams(("parallel", "parallel", "arbitrary")),
        name="ret_scan",
    )(proj, proj, proj, proj, proj, cos, sin, rd, h0)


ATT_TK = 256


def _attn_prep_kernel(q_ref, qr_ref, k_ref, kr_ref, v_ref, cos_ref, sin_ref, qt_ref, ko_ref, vt_ref):
    cos = cos_ref[...]
    sin = sin_ref[...]
    q = (q_ref[...] * cos + qr_ref[...] * sin) * (DIFF_HEAD_DIM ** -0.5)
    qt_ref[...] = q.T.astype(bf16)
    ko_ref[...] = (k_ref[...] * cos + kr_ref[...] * sin).astype(bf16)
    vt_ref[...] = v_ref[...].T.astype(bf16)


def _attn_prep_call(proj, cos, sin):
    bsz, t, _ = proj.shape
    tp = ATT_TK
    nt = t // tp
    col = lambda c0: pl.BlockSpec((None, tp, LANES_V7X), lambda b, h, i: (b, i, c0 + h))
    tab = pl.BlockSpec((tp, LANES_V7X), lambda b, h, i: (i, 0))
    return pl.pallas_call(
        _attn_prep_kernel,
        grid=(bsz, DIFF_HEADS, nt),
        in_specs=[col(COL_DQ), col(COL_DQR), col(COL_DK), col(COL_DKR), col(COL_DV), tab, tab],
        out_specs=[
            pl.BlockSpec((None, None, LANES_V7X, tp), lambda b, h, i: (b, h, 0, i)),
            pl.BlockSpec((None, tp, LANES_V7X), lambda b, h, i: (b, i, h)),
            pl.BlockSpec((None, None, None, DIFF_V_DIM, tp), lambda b, h, i: (b, h, i, 0, 0)),
        ],
        out_shape=[
            jax.ShapeDtypeStruct((bsz, DIFF_HEADS, 2 * DIFF_HEAD_DIM, t), bf16),
            jax.ShapeDtypeStruct((bsz, t, DIFF_WIDTH), bf16),
            jax.ShapeDtypeStruct((bsz, DIFF_HEADS, nt, DIFF_V_DIM, tp), bf16),
        ],
        compiler_params=_cparams(("parallel", "parallel", "parallel")),
        name="attn_prep",
    )(proj, proj, proj, proj, proj, cos, sin)


def _attn_kernel(qt_ref, k_ref, vt_ref, lam_ref, nw_ref, o_ref, m_ref, l_ref, acc_ref, *, nk, lam_init):
    tq = qt_ref.shape[1]
    tk = ATT_TK
    qt = qt_ref[...]
    row = lax.broadcasted_iota(jnp.int32, qt.shape, 0)
    qts = (jnp.where(row < DIFF_HEAD_DIM, qt, jnp.zeros_like(qt)),
           jnp.where(row < DIFF_HEAD_DIM, jnp.zeros_like(qt), qt))
    m_ref[...] = jnp.full(m_ref.shape, NEG_BIG, f32)
    l_ref[...] = jnp.zeros(l_ref.shape, f32)
    acc_ref[...] = jnp.zeros(acc_ref.shape, f32)

    def body(c, carry):
        kc = k_ref[pl.ds(pl.multiple_of(c * tk, tk), tk), :]
        vt = vt_ref[c]
        for mi in range(2):
            s = jnp.dot(kc, qts[mi], preferred_element_type=f32)
            m_old = m_ref[mi]
            m_new = jnp.maximum(m_old, jnp.max(s, axis=0, keepdims=True))
            p = jnp.exp(s - m_new)
            alpha = jnp.exp(m_old - m_new)
            l_ref[mi] = alpha * l_ref[mi] + jnp.sum(p, axis=0, keepdims=True)
            acc_ref[mi] = alpha * acc_ref[mi] + jnp.dot(vt, p.astype(bf16), preferred_element_type=f32)
            m_ref[mi] = m_new
        return carry

    lax.fori_loop(0, nk, body, 0)

    lp = lam_ref[...]
    lam = (jnp.exp(jnp.sum(lp[0:1] * lp[1:2], axis=1, keepdims=True))
           - jnp.exp(jnp.sum(lp[2:3] * lp[3:4], axis=1, keepdims=True)) + lam_init)
    ot = acc_ref[0] / l_ref[0] - lam * (acc_ref[1] / l_ref[1])
    o = ot.T
    ms = jnp.mean(o * o, axis=-1, keepdims=True)
    o_ref[...] = o * lax.rsqrt(ms + EPS) * nw_ref[...] * (1.0 - lam_init)


def _attn_call(qt, k_all, vt_all, lam_p, norm_w, lam_init):
    bsz, nh, _, sq = qt.shape
    sk = k_all.shape[1]
    nk = sk // ATT_TK
    tq = min(512, sq)
    return pl.pallas_call(
        functools.partial(_attn_kernel, nk=nk, lam_init=lam_init),
        grid=(bsz, nh, sq // tq),
        in_specs=[
            pl.BlockSpec((None, None, 2 * DIFF_HEAD_DIM, tq), lambda b, h, i: (b, h, 0, i)),
            pl.BlockSpec((None, sk, LANES_V7X), lambda b, h, i: (b, 0, h)),
            pl.BlockSpec((None, None, nk, DIFF_V_DIM, ATT_TK), lambda b, h, i: (b, h, 0, 0, 0)),
            pl.BlockSpec((4, DIFF_HEAD_DIM), lambda b, h, i: (0, 0)),
            pl.BlockSpec((1, DIFF_V_DIM), lambda b, h, i: (0, 0)),
        ],
        out_specs=pl.BlockSpec((None, tq, DIFF_V_DIM), lambda b, h, i: (b, i, h)),
        out_shape=jax.ShapeDtypeStruct((bsz, sq, DIFF_WIDTH), f32),
        scratch_shapes=[
            pltpu.VMEM((2, 1, tq), f32),
            pltpu.VMEM((2, 1, tq), f32),
            pltpu.VMEM((2, DIFF_V_DIM, tq), f32),
        ],
        compiler_params=_cparams(("parallel", "parallel", "parallel")),
        name="diff_attn",
    )(qt, k_all, vt_all, lam_p, norm_w.reshape(1, DIFF_V_DIM))


def _mix_out_kernel(x_ref, ysf_ref, ysb_ref, xs_ref, z_ref, yd_ref, yrf_ref, yrb_ref, g_ref,
                    dsk_ref, snw_ref, rnw_ref, wo_ref, ga_ref, lw_ref, lb_ref, o_ref, *, alpha):
    y = ysf_ref[...] + ysb_ref[...] + xs_ref[...] * dsk_ref[...]
    y = y * _silu(z_ref[...])
    y_ssd = y * lax.rsqrt(jnp.mean(y * y, axis=-1, keepdims=True) + EPS) * snw_ref[...]
    yr = yrf_ref[...] + yrb_ref[...]
    gate = _silu(g_ref[...])
    rets = []
    for h in range(RET_HEADS):
        yh = yr[:, h * RET_V_DIM:(h + 1) * RET_V_DIM]
        mu = jnp.mean(yh, axis=-1, keepdims=True)
        yc = yh - mu
        var = jnp.mean(yc * yc, axis=-1, keepdims=True)
        rets.append(yc * lax.rsqrt(var + EPS) * rnw_ref[...])
    y_ret = jnp.concatenate(rets, axis=1) * gate
    ycat = jnp.concatenate([y_ssd.astype(bf16), yd_ref[...].astype(bf16), y_ret.astype(bf16)], axis=1)
    mixed = jnp.dot(ycat, wo_ref[...], preferred_element_type=f32)
    o_ref[...] = _layernorm(alpha * x_ref[...] + ga_ref[...] * mixed, lw_ref[...], lb_ref[...])


def _mix_out_call(x, y_ssd, xbc, proj, y_diff, y_ret, d_skip, ssd_norm_w, ret_norm_w, w_out, g_a, ln_w, ln_b, alpha):
    bsz, t, d = x.shape
    tm = min(512, t)
    row = lambda width, c: pl.BlockSpec((None, tm, width), lambda b, i: (b, i, c))
    two = lambda width, dd: pl.BlockSpec((None, None, tm, width), lambda b, i: (b, dd, i, 0))
    vec = lambda width: pl.BlockSpec((1, width), lambda b, i: (0, 0))
    return pl.pallas_call(
        functools.partial(_mix_out_kernel, alpha=alpha),
        grid=(bsz, t // tm),
        in_specs=[
            row(d, 0),
            two(SSD_INNER, 0), two(SSD_INNER, 1),
            row(SSD_INNER, 0),
            row(SSD_INNER, COL_Z * LANES_V7X // SSD_INNER),
            row(DIFF_WIDTH, 0),
            two(RET_WIDTH, 0), two(RET_WIDTH, 1),
            row(RET_WIDTH, COL_RG * LANES_V7X // RET_WIDTH),
            vec(SSD_INNER), vec(SSD_INNER), vec(RET_V_DIM),
            pl.BlockSpec((MIX_WIDTH, d), lambda b, i: (0, 0)),
            pl.BlockSpec((None, 1, d), lambda b, i: (b, 0, 0)),
            vec(d), vec(d),
        ],
        out_specs=row(d, 0),
        out_shape=jax.ShapeDtypeStruct((bsz, t, d), f32),
        compiler_params=_cparams(("parallel", "parallel")),
        name="mix_out",
    )(x, y_ssd, y_ssd, xbc, proj, y_diff, y_ret, y_ret, proj,
      d_skip, ssd_norm_w.reshape(1, -1), ret_norm_w.reshape(1, -1), w_out, g_a, ln_w.reshape(1, -1), ln_b.reshape(1, -1))


def _gelu_exact(x):
    return 0.5 * x * (1.0 + lax.erf(x * (2.0 ** -0.5)))


def _ffn_kernel(x_ref, xp_ref, xn_ref, sc_ref, sh_ref, gf_ref, wu_ref, wv_ref, cw_ref, cb_ref, wd_ref,
                lw_ref, lb_ref, o_ref, xe_ref, ue_ref, acc_ref, *, tm, nt, nf, alpha):
    i = pl.program_id(1)
    c = pl.program_id(2)
    h = SUBLANES_V7X

    @pl.when(c == 0)
    def _():
        scale = 1.0 + sc_ref[...]
        shift = sh_ref[...]
        xe_ref[0:h, :] = jnp.where(i > 0, xp_ref[...] * scale + shift, 0.0)
        xe_ref[h:h + tm, :] = x_ref[...] * scale + shift
        xe_ref[h + tm:2 * h + tm, :] = jnp.where(i < nt - 1, xn_ref[...] * scale + shift, 0.0)
        acc_ref[...] = jnp.zeros(acc_ref.shape, f32)

    ue_ref[...] = jnp.dot(xe_ref[...].astype(bf16), wu_ref[...], preferred_element_type=f32)
    v = jnp.dot(xe_ref[h:h + tm, :].astype(bf16), wv_ref[...], preferred_element_type=f32)
    p = FFN_CONV // 2
    u = cb_ref[...] + ue_ref[h - p:h - p + tm, :] * cw_ref[0:1, :]
    for j in range(1, FFN_CONV):
        u = u + ue_ref[h - p + j:h - p + j + tm, :] * cw_ref[j:j + 1, :]
    gated = (_gelu_exact(u) * v).astype(bf16)
    acc_ref[...] += jnp.dot(gated, wd_ref[...], preferred_element_type=f32)

    @pl.when(c == nf - 1)
    def _():
        o_ref[...] = _layernorm(alpha * x_ref[...] + gf_ref[...] * acc_ref[...], lw_ref[...], lb_ref[...])


def _ffn_call(x, sc, sh, gf, w_up, conv_w, conv_b, w_down, ln_w, ln_b, alpha):
    bsz, t, d = x.shape
    tm = min(512, t)
    nt = t // tm
    tf = D_FF // 2
    nf = D_FF // tf
    hb = tm // SUBLANES_V7X
    last_hb = t // SUBLANES_V7X - 1
    mod = pl.BlockSpec((None, 1, d), lambda b, i, c: (b, 0, 0))
    vec = pl.BlockSpec((1, d), lambda b, i, c: (0, 0))
    return pl.pallas_call(
        functools.partial(_ffn_kernel, tm=tm, nt=nt, nf=nf, alpha=alpha),
        grid=(bsz, nt, nf),
        in_specs=[
            pl.BlockSpec((None, tm, d), lambda b, i, c: (b, i, 0)),
            pl.BlockSpec((None, SUBLANES_V7X, d), lambda b, i, c: (b, jnp.maximum(i * hb - 1, 0), 0)),
            pl.BlockSpec((None, SUBLANES_V7X, d), lambda b, i, c: (b, jnp.minimum((i + 1) * hb, last_hb), 0)),
            mod, mod, mod,
            pl.BlockSpec((d, tf), lambda b, i, c: (0, c)),
            pl.BlockSpec((d, tf), lambda b, i, c: (0, nf + c)),
            pl.BlockSpec((FFN_CONV, tf), lambda b, i, c: (0, c)),
            pl.BlockSpec((1, tf), lambda b, i, c: (0, c)),
            pl.BlockSpec((tf, d), lambda b, i, c: (c, 0)),
            vec, vec,
        ],
        out_specs=pl.BlockSpec((None, tm, d), lambda b, i, c: (b, i, 0)),
        out_shape=jax.ShapeDtypeStruct((bsz, t, d), f32),
        scratch_shapes=[
            pltpu.VMEM((tm + 2 * SUBLANES_V7X, d), f32),
            pltpu.VMEM((tm + 2 * SUBLANES_V7X, tf), f32),
            pltpu.VMEM((tm, d), f32),
        ],
        compiler_params=_cparams(("parallel", "parallel", "arbitrary")),
        name="conv_ffn",
    )(x, x, x, sc, sh, gf, w_up, w_up, conv_w, conv_b.reshape(1, -1), w_down, ln_w.reshape(1, -1), ln_b.reshape(1, -1))


def _rot_cols(w, block):
    d, n = w.shape
    w4 = w.reshape(d, n // block, 2, block // 2)
    return jnp.concatenate([-w4[:, :, 1], w4[:, :, 0]], axis=-1).reshape(d, n)


def _ext_weight(w_in):
    d = w_in.shape[0]
    o = 0
    z = w_in[:, o:o + SSD_INNER]; o += SSD_INNER
    xbc = w_in[:, o:o + SSD_XBC]; o += SSD_XBC
    dt = w_in[:, o:o + 2 * SSD_HEADS]; o += 2 * SSD_HEADS
    dq = w_in[:, o:o + DIFF_QK]; o += DIFF_QK
    dk = w_in[:, o:o + DIFF_QK]; o += DIFF_QK
    dv = w_in[:, o:o + DIFF_WIDTH]; o += DIFF_WIDTH
    rq = w_in[:, o:o + RET_QK]; o += RET_QK
    rk = w_in[:, o:o + RET_QK]; o += RET_QK
    rv = w_in[:, o:o + RET_WIDTH]; o += RET_WIDTH
    rg = w_in[:, o:o + RET_WIDTH]; o += RET_WIDTH
    ax = DIFF_HEAD_DIM // 2
    dt_pad = jnp.zeros((d, LANES_V7X - SSD_HEADS), w_in.dtype)
    ext = jnp.concatenate([z, xbc, dq, _rot_cols(dq, ax), dk, _rot_cols(dk, ax), dv,
                           rq, _rot_cols(rq, RET_QK_DIM), rk, _rot_cols(rk, RET_QK_DIM), rv, rg,
                           dt[:, :SSD_HEADS], dt_pad, dt[:, SSD_HEADS:], dt_pad], axis=1)
    return ext.astype(bf16)


def _rope_tables(s, n_ctx):
    rows = s // GRID_W
    row = jnp.repeat(jnp.arange(rows, dtype=f32), GRID_W)
    col = jnp.tile(jnp.arange(GRID_W, dtype=f32), rows)
    n_ax = DIFF_HEAD_DIM // 4
    inv_ax = 1.0 / (ROPE_BASE ** (jnp.arange(n_ax, dtype=f32) / n_ax))
    ar = row[:, None] * inv_ax
    ac = col[:, None] * inv_ax
    ang = jnp.concatenate([ar, ar, ac, ac], axis=-1)
    ang = jnp.tile(ang, (1, 2))
    inv_ret = 1.0 / (ROPE_BASE ** jnp.linspace(0.0, 1.0, RET_QK_DIM // 2, dtype=f32))
    ang_c = jnp.arange(n_ctx, dtype=f32)[:, None] * inv_ret
    ang_l = (n_ctx + jnp.arange(s, dtype=f32))[:, None] * inv_ret
    tile_ret = lambda a: jnp.tile(jnp.concatenate([a, a], axis=-1), (1, RET_HEADS))
    return {
        "diff_l": (jnp.cos(ang), jnp.sin(ang)),
        "diff_c": (jnp.ones((n_ctx, LANES_V7X), f32), jnp.zeros((n_ctx, LANES_V7X), f32)),
        "ret_l": (jnp.cos(tile_ret(ang_l)), jnp.sin(tile_ret(ang_l))),
        "ret_c": (jnp.cos(tile_ret(ang_c)), jnp.sin(tile_ret(ang_c))),
    }


def kernel(x, c, ctx, c_ctx, w_ada, b_ada, w_in, ssd_conv_w, ssd_conv_b, ssd_a_log, ssd_dt_bias, ssd_d, ssd_norm_w,
           diff_lambda, diff_norm_w, ret_decay, ret_norm_w, w_out, ln1_w, ln1_b, ffn_w_up, ffn_conv_w, ffn_conv_b,
           ffn_w_down, ln2_w, ln2_b):
    bsz, s, d = x.shape
    n_ctx = ctx.shape[1]
    depth = w_ada.shape[0]
    alpha = (2.0 * depth) ** 0.25
    assert d == D_MODEL and s % 512 == 0 and n_ctx % ATT_TK == 0 and s % GRID_W == 0

    tabs = _rope_tables(s, n_ctx)
    nrow = -(-(bsz + 1) // SUBLANES_V7X) * SUBLANES_V7X
    cvecs = jnp.zeros((nrow, d), f32).at[:bsz].set(c).at[bsz].set(c_ctx)
    mod_all = _ada_call(cvecs, w_ada, b_ada)

    gw = SSD_HPG * SSD_HEAD_DIM
    zero_ssd = jnp.zeros((bsz, 2, SSD_GROUPS, SSD_STATE, gw), f32)
    zero_ret = jnp.zeros((bsz, 2, RET_HEADS // 2, 2 * RET_QK_DIM, RET_V_DIM), f32)

    xc = ctx
    for li in range(depth):
        last = li == depth - 1
        lam_init = 0.8 - 0.6 * math.exp(-0.3 * li)
        mod = mod_all[li]
        m_lat = [mod[:bsz, k * d:(k + 1) * d].reshape(bsz, 1, d) for k in range(6)]
        m_ctx = [jnp.broadcast_to(mod[bsz, k * d:(k + 1) * d].reshape(1, 1, d), (bsz, 1, d)) for k in range(6)]
        w_ext = _ext_weight(w_in[li])
        w_out_b = w_out[li].astype(bf16)
        w_up_b = ffn_w_up[li].astype(bf16)
        w_down_b = ffn_w_down[li].astype(bf16)
        d_skip = jnp.repeat(ssd_d[li], SSD_HEAD_DIM).reshape(1, SSD_INNER)

        proj = _modmm_call(x, m_lat[1], m_lat[0], w_ext)
        proj_c = _modmm_call(xc, m_ctx[1], m_ctx[0], w_ext)

        xbc_c = _ssd_conv_call(proj_c, ssd_conv_w[li], ssd_conv_b[li])
        ys_c, hs = _ssd_scan_call(xbc_c, proj_c, ssd_a_log[li], ssd_dt_bias[li], zero_ssd)
        xbc = _ssd_conv_call(proj, ssd_conv_w[li], ssd_conv_b[li])
        ys, _ = _ssd_scan_call(xbc, proj, ssd_a_log[li], ssd_dt_bias[li], hs)

        yr_c, hr = _ret_scan_call(proj_c, *tabs["ret_c"], ret_decay[li], zero_ret)
        yr, _ = _ret_scan_call(proj, *tabs["ret_l"], ret_decay[li], hr)

        qt_c, k_c, vt_c = _attn_prep_call(proj_c, *tabs["diff_c"])
        qt_l, k_l, vt_l = _attn_prep_call(proj, *tabs["diff_l"])
        k_all = jnp.concatenate([k_l, k_c], axis=1)
        vt_all = jnp.concatenate([vt_l, vt_c], axis=2)
        y_diff = _attn_call(qt_l, k_all, vt_all, diff_lambda[li], diff_norm_w[li], lam_init)

        x = _mix_out_call(x, ys, xbc, proj, y_diff, yr, d_skip, ssd_norm_w[li], ret_norm_w[li], w_out_b,
                          m_lat[2], ln1_w[li], ln1_b[li], alpha)
        x = _ffn_call(x, m_lat[4], m_lat[3], m_lat[5], w_up_b, ffn_conv_w[li], ffn_conv_b[li], w_down_b,
                      ln2_w[li], ln2_b[li], alpha)
        if not last:
            yc_diff = _attn_call(qt_c, k_c, vt_c, diff_lambda[li], diff_norm_w[li], lam_init)
            xc = _mix_out_call(xc, ys_c, xbc_c, proj_c, yc_diff, yr_c, d_skip, ssd_norm_w[li], ret_norm_w[li], w_out_b,
                               m_ctx[2], ln1_w[li], ln1_b[li], alpha)
            xc = _ffn_call(xc, m_ctx[4], m_ctx[3], m_ctx[5], w_up_b, ffn_conv_w[li], ffn_conv_b[li], w_down_b,
                           ln2_w[li], ln2_b[li], alpha)
    return x
```

```python
import functools
import math

import jax
import jax.numpy as jnp
from jax import lax
from jax.experimental import pallas as pl
from jax.experimental.pallas import tpu as pltpu

f32 = jnp.float32
bf16 = jnp.bfloat16
HIGHEST = lax.Precision.HIGHEST

D_MODEL = 1024
GRID_W = 64
CHUNK = 128
ROPE_BASE = 10000.0
SSD_INNER = D_MODEL
SSD_HEAD_DIM = 64
SSD_HEADS = SSD_INNER // SSD_HEAD_DIM
SSD_GROUPS = 2
SSD_HPG = SSD_HEADS // SSD_GROUPS
SSD_STATE = 128
SSD_CONV = 5
SSD_XBC = SSD_INNER + 2 * SSD_GROUPS * SSD_STATE
SSD_COLS = SSD_INNER + SSD_XBC + 2 * SSD_HEADS
DIFF_WIDTH = D_MODEL // 2
DIFF_V_DIM = 128
DIFF_HEADS = DIFF_WIDTH // DIFF_V_DIM
DIFF_HEAD_DIM = DIFF_V_DIM // 2
DIFF_QK = DIFF_HEADS * 2 * DIFF_HEAD_DIM
DIFF_COLS = 2 * DIFF_QK + DIFF_WIDTH
RET_WIDTH = D_MODEL // 2
RET_V_DIM = 128
RET_HEADS = RET_WIDTH // RET_V_DIM
RET_QK_DIM = RET_V_DIM // 2
RET_QK = RET_HEADS * RET_QK_DIM
RET_COLS = 2 * RET_QK + 2 * RET_WIDTH
MIX_WIDTH = 2 * D_MODEL
D_FF = 11 * D_MODEL // 4
FFN_CONV = 3
EPS = 1e-5

LANES_V7X = 128
SUBLANES_V7X = 8
VMEM_LIMIT_V7X = 56 * 1024 * 1024

COL_Z = 0
COL_XBC = 8
COL_DQ = 20
COL_DQR = 24
COL_DK = 28
COL_DKR = 32
COL_DV = 36
COL_RQ = 40
COL_RQR = 42
COL_RK = 44
COL_RKR = 46
COL_RV = 48
COL_RG = 52
N_MAIN = 56 * LANES_V7X
NEG_BIG = -1e30


def _cparams(sem):
    return pltpu.CompilerParams(dimension_semantics=sem, vmem_limit_bytes=VMEM_LIMIT_V7X)


def _silu(x):
    return x * (1.0 / (1.0 + jnp.exp(-x)))


def _layernorm(x, w, b):
    mu = jnp.mean(x, axis=-1, keepdims=True)
    xc = x - mu
    var = jnp.mean(xc * xc, axis=-1, keepdims=True)
    return xc * lax.rsqrt(var + EPS) * w + b


def _split2(x):
    hi = x.astype(bf16)
    lo = (x - hi.astype(f32)).astype(bf16)
    return hi, lo


def _ada_kernel(c_ref, w_ref, b_ref, o_ref):
    c = c_ref[...]
    o_ref[...] = jnp.dot(_silu(c), w_ref[...], precision=HIGHEST, preferred_element_type=f32) + b_ref[...]


def _ada_call(cvecs, w_ada, b_ada):
    depth, d, n = w_ada.shape
    r = cvecs.shape[0]
    tn = 1536
    return pl.pallas_call(
        _ada_kernel,
        grid=(depth, n // tn),
        in_specs=[
            pl.BlockSpec((r, d), lambda l, j: (0, 0)),
            pl.BlockSpec((None, d, tn), lambda l, j: (l, 0, j)),
            pl.BlockSpec((None, 1, tn), lambda l, j: (l, 0, j)),
        ],
        out_specs=pl.BlockSpec((None, r, tn), lambda l, j: (l, 0, j)),
        out_shape=jax.ShapeDtypeStruct((depth, r, n), f32),
        compiler_params=_cparams(("parallel", "parallel")),
        name="ada",
    )(cvecs, w_ada, b_ada.reshape(depth, 1, n))


def _modmm_kernel(x_ref, sc_ref, sh_ref, w_ref, o_ref):
    xm = x_ref[...] * (1.0 + sc_ref[...]) + sh_ref[...]
    o_ref[...] = jnp.dot(xm.astype(bf16), w_ref[...], preferred_element_type=f32).astype(o_ref.dtype)


def _modmm_call(x, sc, sh, w_ext, out_dtype, n_col_tiles):
    bsz, t, d = x.shape
    n = w_ext.shape[1]
    tm = min(512, t)
    tn = n // n_col_tiles
    return pl.pallas_call(
        _modmm_kernel,
        grid=(n // tn, bsz, t // tm),
        in_specs=[
            pl.BlockSpec((None, tm, d), lambda j, b, i: (b, i, 0)),
            pl.BlockSpec((None, 1, d), lambda j, b, i: (b, 0, 0)),
            pl.BlockSpec((None, 1, d), lambda j, b, i: (b, 0, 0)),
            pl.BlockSpec((d, tn), lambda j, b, i: (0, j)),
        ],
        out_specs=pl.BlockSpec((None, tm, tn), lambda j, b, i: (b, i, j)),
        out_shape=jax.ShapeDtypeStruct((bsz, t, n), out_dtype),
        compiler_params=_cparams(("parallel", "parallel", "parallel")),
        name="in_proj",
    )(x, sc, sh, w_ext)


def _ssd_conv_kernel(cur_ref, prev_ref, next_ref, w_ref, b_ref, o_ref, ext_ref, *, tm, nt):
    i = pl.program_id(1)
    h = SUBLANES_V7X
    ext_ref[0:h, :] = jnp.where(i > 0, prev_ref[...].astype(f32)[h:2 * h], 0.0)
    ext_ref[h:h + tm, :] = cur_ref[...].astype(f32)
    ext_ref[h + tm:2 * h + tm, :] = jnp.where(i < nt - 1, next_ref[...].astype(f32)[0:h], 0.0)
    p = SSD_CONV // 2
    acc = ext_ref[h - p:h - p + tm, :] * w_ref[0:1, :]
    for j in range(1, SSD_CONV):
        acc = acc + ext_ref[h - p + j:h - p + j + tm, :] * w_ref[j:j + 1, :]
    o_ref[...] = _silu(acc + b_ref[...]).astype(o_ref.dtype)


def _ssd_conv_call(proj, conv_w, conv_b):
    bsz, t, _ = proj.shape
    tm = min(512, t)
    nt = t // tm
    cw = 512
    nc = SSD_XBC // cw
    c0 = COL_XBC * LANES_V7X // cw
    halo = 2 * SUBLANES_V7X
    hb = tm // halo
    last_hb = t // halo - 1
    return pl.pallas_call(
        functools.partial(_ssd_conv_kernel, tm=tm, nt=nt),
        grid=(bsz, nt, nc),
        in_specs=[
            pl.BlockSpec((None, tm, cw), lambda b, i, c: (b, i, c0 + c)),
            pl.BlockSpec((None, halo, cw), lambda b, i, c: (b, jnp.maximum(i * hb - 1, 0), c0 + c)),
            pl.BlockSpec((None, halo, cw), lambda b, i, c: (b, jnp.minimum((i + 1) * hb, last_hb), c0 + c)),
            pl.BlockSpec((SSD_CONV, cw), lambda b, i, c: (0, c)),
            pl.BlockSpec((1, cw), lambda b, i, c: (0, c)),
        ],
        out_specs=pl.BlockSpec((None, tm, cw), lambda b, i, c: (b, i, c)),
        out_shape=jax.ShapeDtypeStruct((bsz, t, SSD_XBC), bf16),
        scratch_shapes=[pltpu.VMEM((tm + 2 * SUBLANES_V7X, cw), f32)],
        compiler_params=_cparams(("parallel", "parallel", "parallel")),
        name="ssd_conv",
    )(proj, proj, proj, conv_w, conv_b.reshape(1, SSD_XBC))


def _ssd_scan_kernel(cq_ref, bk_ref, x_ref, dt_ref, alog_ref, dtb_ref, tri_ref, e_ref, h0_ref,
                     y_ref, hout_ref, h_ref, *, ns, cps):
    d = pl.program_id(1)
    j = pl.program_id(2)
    L = CHUNK
    gw = SSD_HPG * SSD_HEAD_DIM

    @pl.when(j == 0)
    def _():
        h_ref[...] = h0_ref[...]

    tri = tri_ref[...]
    mask = tri > 0.5
    a_neg = -jnp.exp(alog_ref[...])
    lane = lax.broadcasted_iota(jnp.int32, (L, LANES_V7X), 1)
    lo_half = lane < SSD_HEAD_DIM

    def one_chunk(rows):
        sp = jax.nn.softplus(dt_ref[rows, :] + dtb_ref[...])
        la = sp * a_neg
        cum = jnp.dot(tri, la, precision=HIGHEST, preferred_element_type=f32)
        cum_t = cum.T
        sp_t = sp.T
        tot = jnp.sum(la, axis=0, keepdims=True)
        w_end = jnp.exp(tot - cum) * sp
        e_cum = jnp.exp(cum)
        dec = jnp.broadcast_to(jnp.exp(tot), (SUBLANES_V7X, LANES_V7X))
        pieces = _split2(w_end) + _split2(e_cum) + _split2(dec)
        r = jnp.dot(jnp.concatenate(pieces, axis=0), e_ref[...], preferred_element_type=f32)
        w_x = r[0:L] + r[L:2 * L]
        ec_x = r[2 * L:3 * L] + r[3 * L:4 * L]
        dec_x = r[4 * L:4 * L + 1] + r[4 * L + SUBLANES_V7X:4 * L + SUBLANES_V7X + 1]

        x = x_ref[rows, :].astype(f32)
        xw = (x * w_x).astype(bf16)
        for g in range(SSD_GROUPS):
            qg = cq_ref[rows, g * SSD_STATE:(g + 1) * SSD_STATE]
            kg = bk_ref[rows, g * SSD_STATE:(g + 1) * SSD_STATE]
            scores = lax.dot_general(qg, kg, (((1,), (1,)), ((), ())), preferred_element_type=f32)
            kg_t = kg.astype(f32).T.astype(bf16)
            h_old = h_ref[g]
            y_off = jnp.dot(qg, h_old.astype(bf16), preferred_element_type=f32) * ec_x[:, g * gw:(g + 1) * gw]
            h_ref[g] = dec_x[:, g * gw:(g + 1) * gw] * h_old + jnp.dot(
                kg_t, xw[:, g * gw:(g + 1) * gw], preferred_element_type=f32)
            outs = []
            for pp in range(SSD_HPG // 2):
                ms = []
                for hh in range(2):
                    head = g * SSD_HPG + 2 * pp + hh
                    seg = cum[:, head:head + 1] - cum_t[head:head + 1, :]
                    decay = jnp.exp(jnp.where(mask, seg, NEG_BIG))
                    ms.append((scores * decay * sp_t[head:head + 1, :]).astype(bf16))
                c_lo = g * gw + pp * LANES_V7X
                xp = x[:, c_lo:c_lo + LANES_V7X]
                rhs = jnp.concatenate([jnp.where(lo_half, xp, 0.0), jnp.where(lo_half, 0.0, xp)],
                                      axis=0).astype(bf16)
                outs.append(jnp.dot(jnp.concatenate(ms, axis=1), rhs, preferred_element_type=f32))
            y_ref[rows, g * gw:(g + 1) * gw] = (jnp.concatenate(outs, axis=1) + y_off).astype(y_ref.dtype)

    for t in range(cps):
        off = (t + d * (cps - 1 - 2 * t)) * L
        one_chunk(pl.ds(pl.multiple_of(off, L), L))

    @pl.when(j == ns - 1)
    def _():
        hout_ref[...] = h_ref[...]


def _chunks_per_step(nc):
    return 2 if nc % 2 == 0 else 1


def _ssd_scan_call(xbc, dt_raw, a_log, dt_bias, h0):
    bsz, t, _ = xbc.shape
    cps = _chunks_per_step(t // CHUNK)
    L = CHUNK
    rows = cps * L
    ns = t // rows
    gw = SSD_HPG * SSD_HEAD_DIM

    def chunk(d, j):
        return j + d * (ns - 1 - 2 * j)

    alog = jnp.zeros((2, 1, LANES_V7X), f32).at[:, 0, :SSD_HEADS].set(a_log)
    dtb = jnp.zeros((2, 1, LANES_V7X), f32).at[:, 0, :SSD_HEADS].set(dt_bias)
    idx = jnp.arange(L)
    tri = jnp.stack([(idx[None, :] <= idx[:, None]), (idx[None, :] >= idx[:, None])]).astype(f32)
    head_col = jnp.arange(LANES_V7X)[:, None]
    heads = (jnp.arange(SSD_INNER) // SSD_HEAD_DIM)[None, :]
    expand = (head_col == heads).astype(bf16)
    bcol = SSD_INNER // (SSD_GROUPS * SSD_STATE)
    return pl.pallas_call(
        functools.partial(_ssd_scan_kernel, ns=ns, cps=cps),
        grid=(bsz, 2, ns),
        in_specs=[
            pl.BlockSpec((None, rows, SSD_GROUPS * SSD_STATE), lambda b, d, j: (b, chunk(d, j), bcol + 1)),
            pl.BlockSpec((None, rows, SSD_GROUPS * SSD_STATE), lambda b, d, j: (b, chunk(d, j), bcol)),
            pl.BlockSpec((None, rows, SSD_INNER), lambda b, d, j: (b, chunk(d, j), 0)),
            pl.BlockSpec((None, rows, LANES_V7X), lambda b, d, j: (b, chunk(d, j), d)),
            pl.BlockSpec((None, 1, LANES_V7X), lambda b, d, j: (d, 0, 0)),
            pl.BlockSpec((None, 1, LANES_V7X), lambda b, d, j: (d, 0, 0)),
            pl.BlockSpec((None, L, L), lambda b, d, j: (d, 0, 0)),
            pl.BlockSpec((LANES_V7X, SSD_INNER), lambda b, d, j: (0, 0)),
            pl.BlockSpec((None, None, SSD_GROUPS, SSD_STATE, gw), lambda b, d, j: (b, d, 0, 0, 0)),
        ],
        out_specs=[
            pl.BlockSpec((None, None, rows, SSD_INNER), lambda b, d, j: (b, d, chunk(d, j), 0)),
            pl.BlockSpec((None, None, SSD_GROUPS, SSD_STATE, gw), lambda b, d, j: (b, d, 0, 0, 0)),
        ],
        out_shape=[
            jax.ShapeDtypeStruct((bsz, 2, t, SSD_INNER), bf16),
            jax.ShapeDtypeStruct((bsz, 2, SSD_GROUPS, SSD_STATE, gw), f32),
        ],
        scratch_shapes=[pltpu.VMEM((SSD_GROUPS, SSD_STATE, gw), f32)],
        compiler_params=_cparams(("parallel", "parallel", "arbitrary")),
        name="ssd_scan",
    )(xbc, xbc, xbc, dt_raw, alog, dtb, tri, expand, h0)


def _ret_scan_kernel(q_ref, qr_ref, k_ref, kr_ref, v_ref, cos_ref, sin_ref, rd_ref, h0_ref,
                     y_ref, hout_ref, h_ref, dec_ref, ecx_ref, tex_ref, dst_ref, *, ns, cps):
    d = pl.program_id(1)
    j = pl.program_id(2)
    L = CHUNK
    npair = RET_HEADS // 2

    @pl.when(j == 0)
    def _():
        h_ref[...] = h0_ref[...]
        lg = -jnp.exp(rd_ref[...])
        li = lax.broadcasted_iota(jnp.int32, (L, L), 0)
        si = lax.broadcasted_iota(jnp.int32, (L, L), 1)
        dist = jnp.where(d == 0, li - si, si - li)
        causal = dist >= 0
        distf = dist.astype(f32)
        lane = lax.broadcasted_iota(jnp.int32, (1, RET_WIDTH), 1)
        lgx = jnp.zeros((1, RET_WIDTH), f32)
        for h in range(RET_HEADS):
            lgh = lg[:, h:h + 1]
            dec_ref[h] = jnp.exp(jnp.where(causal, distf * lgh, NEG_BIG))
            lgx = jnp.where((lane >= h * RET_V_DIM) & (lane < (h + 1) * RET_V_DIM), lgh, lgx)
        pos = lax.broadcasted_iota(jnp.int32, (L, 1), 0)
        steps = jnp.where(d == 0, pos + 1, L - pos).astype(f32)
        cumx = steps * lgx
        totx = float(L) * lgx
        ecx_ref[...] = jnp.exp(cumx)
        tex_ref[...] = jnp.exp(totx - cumx)
        rowi = lax.broadcasted_iota(jnp.int32, (2 * RET_QK_DIM, RET_V_DIM), 0)
        for pp in range(npair):
            d0 = jnp.exp(float(L) * lg[:, 2 * pp:2 * pp + 1])
            d1 = jnp.exp(float(L) * lg[:, 2 * pp + 1:2 * pp + 2])
            dst_ref[pp] = jnp.where(rowi < RET_QK_DIM, d0, d1)

    ecx = ecx_ref[...]
    tex = tex_ref[...]
    hi_half = lax.broadcasted_iota(jnp.int32, (L, LANES_V7X), 1) >= RET_QK_DIM
    zeros_v = jnp.zeros((L, RET_V_DIM), bf16)

    def one_chunk(rows):
        cos = cos_ref[rows, :]
        sin = sin_ref[rows, :]
        q = q_ref[rows, :].astype(f32) * cos + qr_ref[rows, :].astype(f32) * sin
        k = (k_ref[rows, :].astype(f32) * cos + kr_ref[rows, :].astype(f32) * sin) * (RET_QK_DIM ** -0.5)
        vb = v_ref[rows, :]
        vw = (vb.astype(f32) * tex).astype(bf16)
        for pp in range(npair):
            qp = q[:, pp * LANES_V7X:(pp + 1) * LANES_V7X]
            kp = k[:, pp * LANES_V7X:(pp + 1) * LANES_V7X]
            kp_b = kp.astype(bf16)
            kp_t = kp.T.astype(bf16)
            r_old = h_ref[pp]
            r_b = r_old.astype(bf16)
            ms, offs = [], []
            for hh in range(2):
                h = 2 * pp + hh
                qm = (jnp.where(hi_half, qp, 0.0) if hh else jnp.where(hi_half, 0.0, qp)).astype(bf16)
                scores = lax.dot_general(qm, kp_b, (((1,), (1,)), ((), ())), preferred_element_type=f32)
                ms.append((scores * dec_ref[h]).astype(bf16))
                offs.append(jnp.dot(qm, r_b, preferred_element_type=f32))
            va = vb[:, (2 * pp) * RET_V_DIM:(2 * pp + 1) * RET_V_DIM]
            vc = vb[:, (2 * pp + 1) * RET_V_DIM:(2 * pp + 2) * RET_V_DIM]
            rhs = jnp.concatenate([jnp.concatenate([va, zeros_v], axis=1),
                                   jnp.concatenate([zeros_v, vc], axis=1)], axis=0)
            y_diag = jnp.dot(jnp.concatenate(ms, axis=1), rhs, preferred_element_type=f32)
            c0 = 2 * pp * RET_V_DIM
            y_ref[rows, c0:c0 + 2 * RET_V_DIM] = (
                y_diag + jnp.concatenate(offs, axis=1) * ecx[:, c0:c0 + 2 * RET_V_DIM]).astype(y_ref.dtype)
            upd = jnp.dot(kp_t, vw[:, c0:c0 + 2 * RET_V_DIM], preferred_element_type=f32)
            new = jnp.concatenate([upd[0:RET_QK_DIM, 0:RET_V_DIM],
                                   upd[RET_QK_DIM:2 * RET_QK_DIM, RET_V_DIM:2 * RET_V_DIM]], axis=0)
            h_ref[pp] = dst_ref[pp] * r_old + new

    for t in range(cps):
        off = (t + d * (cps - 1 - 2 * t)) * L
        one_chunk(pl.ds(pl.multiple_of(off, L), L))

    @pl.when(j == ns - 1)
    def _():
        hout_ref[...] = h_ref[...]


def _ret_scan_call(proj, cos, sin, decay_raw, h0):
    bsz, t, _ = proj.shape
    cps = _chunks_per_step(t // CHUNK)
    L = CHUNK
    rows = cps * L
    ns = t // rows
    npair = RET_HEADS // 2

    def chunk(d, j):
        return j + d * (ns - 1 - 2 * j)

    rd = jnp.zeros((2, 1, LANES_V7X), f32).at[:, 0, :RET_HEADS].set(decay_raw)
    qk = lambda c: pl.BlockSpec((None, rows, RET_QK), lambda b, d, j: (b, chunk(d, j), c // 2))
    tab = pl.BlockSpec((rows, RET_QK), lambda b, d, j: (chunk(d, j), 0))
    st = pl.BlockSpec((None, None, npair, 2 * RET_QK_DIM, RET_V_DIM), lambda b, d, j: (b, d, 0, 0, 0))
    return pl.pallas_call(
        functools.partial(_ret_scan_kernel, ns=ns, cps=cps),
        grid=(bsz, 2, ns),
        in_specs=[
            qk(COL_RQ), qk(COL_RQR), qk(COL_RK), qk(COL_RKR),
            pl.BlockSpec((None, rows, RET_WIDTH), lambda b, d, j: (b, chunk(d, j), COL_RV // 4)),
            tab, tab,
            pl.BlockSpec((None, 1, LANES_V7X), lambda b, d, j: (d, 0, 0)),
            st,
        ],
        out_specs=[
            pl.BlockSpec((None, None, rows, RET_WIDTH), lambda b, d, j: (b, d, chunk(d, j), 0)),
            st,
        ],
        out_shape=[
            jax.ShapeDtypeStruct((bsz, 2, t, RET_WIDTH), bf16),
            jax.ShapeDtypeStruct((bsz, 2, npair, 2 * RET_QK_DIM, RET_V_DIM), f32),
        ],
        scratch_shapes=[
            pltpu.VMEM((npair, 2 * RET_QK_DIM, RET_V_DIM), f32),
            pltpu.VMEM((RET_HEADS, L, L), f32),
            pltpu.VMEM((L, RET_WIDTH), f32),
            pltpu.VMEM((L, RET_WIDTH), f32),
            pltpu.VMEM((npair, 2 * RET_QK_DIM, RET_V_DIM), f32),
        ],
        compiler_params=_cparams(("parallel", "parallel", "arbitrary")),
        name="ret_scan",
    )(proj, proj, proj, proj, proj, cos, sin, rd, h0)


ATT_TP = 256
ATT_TQS = 256
ATT_VROWS = DIFF_V_DIM + 16
LOG2E = 1.4426950408889634


def _attn_prep_kernel(q_ref, qr_ref, k_ref, kr_ref, v_ref, cos_ref, sin_ref, qt_ref, ko_ref, vt_ref):
    cos = cos_ref[...]
    sin = sin_ref[...]
    q = (q_ref[...].astype(f32) * cos + qr_ref[...].astype(f32) * sin) * (DIFF_HEAD_DIM ** -0.5 * LOG2E)
    ko_ref[...] = (k_ref[...].astype(f32) * cos + kr_ref[...].astype(f32) * sin).astype(bf16)
    v = v_ref[...].astype(f32)
    ones = jnp.ones((ATT_VROWS - DIFF_V_DIM, vt_ref.shape[2]), bf16)
    for h in range(DIFF_HEADS):
        cols = slice(h * LANES_V7X, (h + 1) * LANES_V7X)
        qt_ref[h] = q[:, cols].T.astype(bf16)
        vt_ref[h, 0:DIFF_V_DIM, :] = v[:, cols].T.astype(bf16)
        vt_ref[h, DIFF_V_DIM:ATT_VROWS, :] = ones


def _attn_prep_call(proj, cos, sin):
    bsz, t, _ = proj.shape
    tp = ATT_TP
    nt = t // tp
    w = DIFF_WIDTH
    col = lambda c0: pl.BlockSpec((None, tp, w), lambda b, i: (b, i, c0 * LANES_V7X // w))
    tab = pl.BlockSpec((tp, w), lambda b, i: (i, 0))
    return pl.pallas_call(
        _attn_prep_kernel,
        grid=(bsz, nt),
        in_specs=[col(COL_DQ), col(COL_DQR), col(COL_DK), col(COL_DKR), col(COL_DV), tab, tab],
        out_specs=[
            pl.BlockSpec((None, DIFF_HEADS, LANES_V7X, tp), lambda b, i: (b, 0, 0, i)),
            pl.BlockSpec((None, tp, w), lambda b, i: (b, i, 0)),
            pl.BlockSpec((None, DIFF_HEADS, ATT_VROWS, tp), lambda b, i: (b, 0, 0, i)),
        ],
        out_shape=[
            jax.ShapeDtypeStruct((bsz, DIFF_HEADS, 2 * DIFF_HEAD_DIM, t), bf16),
            jax.ShapeDtypeStruct((bsz, t, DIFF_WIDTH), bf16),
            jax.ShapeDtypeStruct((bsz, DIFF_HEADS, ATT_VROWS, t), bf16),
        ],
        compiler_params=_cparams(("parallel", "parallel")),
        name="attn_prep",
    )(proj, proj, proj, proj, proj, cos, sin)


def _attn_kernel(qt_ref, k_ref, vt_ref, lam_ref, nw_ref, o_ref, q2_ref, m_ref, acc_ref, s_ref, p_ref, a_ref,
                 *, nk, tk, lam_init):
    tq = qt_ref.shape[1]
    qt = qt_ref[...]
    row = lax.broadcasted_iota(jnp.int32, qt.shape, 0)
    q2_ref[0] = jnp.where(row < DIFF_HEAD_DIM, qt, jnp.zeros_like(qt))
    q2_ref[1] = jnp.where(row < DIFF_HEAD_DIM, jnp.zeros_like(qt), qt)
    m_ref[...] = jnp.full(m_ref.shape, NEG_BIG, f32)
    acc_ref[...] = jnp.zeros(acc_ref.shape, f32)
    chains = [(mi, slice(sb * ATT_TQS, (sb + 1) * ATT_TQS)) for mi in range(2) for sb in range(tq // ATT_TQS)]

    def key_chunk(c):
        return k_ref[pl.ds(pl.multiple_of(c * tk, ATT_TP), tk), :]

    def value_chunk(c):
        return vt_ref[:, pl.ds(pl.multiple_of(c * tk, ATT_TP), tk)]

    def step(c, par, c_next, has_prev, softmax=True):
        oth = 1 - par
        kn = None if c_next is None else key_chunk(c_next)
        vt = value_chunk(c - 1) if has_prev else None
        for mi, cols in chains:
            if kn is not None:
                s_ref[oth, mi, :, cols] = jnp.dot(kn, q2_ref[mi, :, cols], preferred_element_type=f32)
            if has_prev:
                acc_ref[mi, :, cols] = (a_ref[oth, mi, :, cols] * acc_ref[mi, :, cols]
                                        + jnp.dot(vt, p_ref[oth, mi, :, cols], preferred_element_type=f32))
            if softmax:
                s = s_ref[par, mi, :, cols]
                m_old = m_ref[mi, :, cols]
                m_new = jnp.maximum(m_old, jnp.max(s, axis=0, keepdims=True))
                p_ref[par, mi, :, cols] = jnp.exp2(s - m_new).astype(bf16)
                a_ref[par, mi, :, cols] = jnp.exp2(m_old - m_new)
                m_ref[mi, :, cols] = m_new

    k0 = key_chunk(0)
    for mi, cols in chains:
        s_ref[0, mi, :, cols] = jnp.dot(k0, q2_ref[mi, :, cols], preferred_element_type=f32)
    step(0, 0, 1 if nk > 1 else None, False)

    def pair(i, carry):
        step(2 * i + 1, 1, 2 * i + 2, True)
        step(2 * i + 2, 0, jnp.minimum(2 * i + 3, nk - 1), True)
        return carry

    n_pairs = (nk - 1) // 2
    lax.fori_loop(0, n_pairs, pair, 0)
    if (nk - 1) % 2:
        step(nk - 1, 1, None, True)
    step(nk, nk % 2, None, True, softmax=False)

    lp = lam_ref[...]
    lam = (jnp.exp(jnp.sum(lp[0:1] * lp[1:2], axis=1, keepdims=True))
           - jnp.exp(jnp.sum(lp[2:3] * lp[3:4], axis=1, keepdims=True)) + lam_init)
    nv = DIFF_V_DIM
    ot = (acc_ref[0, 0:nv, :] / acc_ref[0, nv:nv + 1, :]
          - lam * (acc_ref[1, 0:nv, :] / acc_ref[1, nv:nv + 1, :]))
    o = ot.T
    ms = jnp.mean(o * o, axis=-1, keepdims=True)
    o_ref[...] = (o * lax.rsqrt(ms + EPS) * nw_ref[...] * (1.0 - lam_init)).astype(o_ref.dtype)


def _attn_call(qt, k_all, vt_all, lam_p, norm_w, lam_init):
    bsz, nh, _, sq = qt.shape
    sk = k_all.shape[1]
    tk = 3 * ATT_TP if sk % (3 * ATT_TP) == 0 else ATT_TP
    nk = sk // tk
    tq = min(1024, sq)
    return pl.pallas_call(
        functools.partial(_attn_kernel, nk=nk, tk=tk, lam_init=lam_init),
        grid=(bsz, nh, sq // tq),
        in_specs=[
            pl.BlockSpec((None, None, 2 * DIFF_HEAD_DIM, tq), lambda b, h, i: (b, h, 0, i)),
            pl.BlockSpec((None, sk, LANES_V7X), lambda b, h, i: (b, 0, h)),
            pl.BlockSpec((None, None, ATT_VROWS, sk), lambda b, h, i: (b, h, 0, 0)),
            pl.BlockSpec((4, DIFF_HEAD_DIM), lambda b, h, i: (0, 0)),
            pl.BlockSpec((1, DIFF_V_DIM), lambda b, h, i: (0, 0)),
        ],
        out_specs=pl.BlockSpec((None, tq, DIFF_V_DIM), lambda b, h, i: (b, i, h)),
        out_shape=jax.ShapeDtypeStruct((bsz, sq, DIFF_WIDTH), bf16),
        scratch_shapes=[
            pltpu.VMEM((2, 2 * DIFF_HEAD_DIM, tq), bf16),
            pltpu.VMEM((2, 1, tq), f32),
            pltpu.VMEM((2, ATT_VROWS, tq), f32),
            pltpu.VMEM((2, 2, tk, tq), f32),
            pltpu.VMEM((2, 2, tk, tq), bf16),
            pltpu.VMEM((2, 2, 1, tq), f32),
        ],
        compiler_params=_cparams(("parallel", "parallel", "parallel")),
        name="diff_attn",
    )(qt, k_all, vt_all, lam_p, norm_w.reshape(1, DIFF_V_DIM))


def _mix_out_kernel(x_ref, ysf_ref, ysb_ref, xs_ref, z_ref, yd_ref, yrf_ref, yrb_ref, g_ref,
                    dsk_ref, snw_ref, rnw_ref, wo_ref, ga_ref, lw_ref, lb_ref, o_ref, *, alpha):
    y = ysf_ref[...].astype(f32) + ysb_ref[...].astype(f32) + xs_ref[...].astype(f32) * dsk_ref[...]
    y = y * _silu(z_ref[...].astype(f32))
    y_ssd = y * lax.rsqrt(jnp.mean(y * y, axis=-1, keepdims=True) + EPS) * snw_ref[...]
    yr = yrf_ref[...].astype(f32) + yrb_ref[...].astype(f32)
    gate = _silu(g_ref[...].astype(f32))
    rets = []
    for h in range(RET_HEADS):
        yh = yr[:, h * RET_V_DIM:(h + 1) * RET_V_DIM]
        mu = jnp.mean(yh, axis=-1, keepdims=True)
        yc = yh - mu
        var = jnp.mean(yc * yc, axis=-1, keepdims=True)
        rets.append(yc * lax.rsqrt(var + EPS) * rnw_ref[...])
    y_ret = jnp.concatenate(rets, axis=1) * gate
    ycat = jnp.concatenate([y_ssd.astype(bf16), yd_ref[...], y_ret.astype(bf16)], axis=1)
    mixed = jnp.dot(ycat, wo_ref[...], preferred_element_type=f32)
    o_ref[...] = _layernorm(alpha * x_ref[...] + ga_ref[...] * mixed, lw_ref[...], lb_ref[...])


def _mix_out_call(x, y_ssd, xbc, proj, y_diff, y_ret, d_skip, ssd_norm_w, ret_norm_w, w_out, g_a, ln_w, ln_b, alpha):
    bsz, t, d = x.shape
    tm = min(512, t)
    row = lambda width, c: pl.BlockSpec((None, tm, width), lambda b, i: (b, i, c))
    two = lambda width, dd: pl.BlockSpec((None, None, tm, width), lambda b, i: (b, dd, i, 0))
    vec = lambda width: pl.BlockSpec((1, width), lambda b, i: (0, 0))
    return pl.pallas_call(
        functools.partial(_mix_out_kernel, alpha=alpha),
        grid=(bsz, t // tm),
        in_specs=[
            row(d, 0),
            two(SSD_INNER, 0), two(SSD_INNER, 1),
            row(SSD_INNER, 0),
            row(SSD_INNER, COL_Z * LANES_V7X // SSD_INNER),
            row(DIFF_WIDTH, 0),
            two(RET_WIDTH, 0), two(RET_WIDTH, 1),
            row(RET_WIDTH, COL_RG * LANES_V7X // RET_WIDTH),
            vec(SSD_INNER), vec(SSD_INNER), vec(RET_V_DIM),
            pl.BlockSpec((MIX_WIDTH, d), lambda b, i: (0, 0)),
            pl.BlockSpec((None, 1, d), lambda b, i: (b, 0, 0)),
            vec(d), vec(d),
        ],
        out_specs=row(d, 0),
        out_shape=jax.ShapeDtypeStruct((bsz, t, d), f32),
        compiler_params=_cparams(("parallel", "parallel")),
        name="mix_out",
    )(x, y_ssd, y_ssd, xbc, proj, y_diff, y_ret, y_ret, proj,
      d_skip, ssd_norm_w.reshape(1, -1), ret_norm_w.reshape(1, -1), w_out, g_a, ln_w.reshape(1, -1), ln_b.reshape(1, -1))


def _gelu_exact(x):
    return 0.5 * x * (1.0 + lax.erf(x * (2.0 ** -0.5)))


def _ffn_kernel(x_ref, xp_ref, xn_ref, sc_ref, sh_ref, gf_ref, wu_ref, wv_ref, cw_ref, cb_ref, wd_ref,
                lw_ref, lb_ref, o_ref, xe_ref, ue_ref, acc_ref, *, tm, nt, nf, alpha):
    i = pl.program_id(1)
    c = pl.program_id(2)
    h = SUBLANES_V7X

    @pl.when(c == 0)
    def _():
        scale = 1.0 + sc_ref[...]
        shift = sh_ref[...]
        xe_ref[0:h, :] = jnp.where(i > 0, xp_ref[...] * scale + shift, 0.0)
        xe_ref[h:h + tm, :] = x_ref[...] * scale + shift
        xe_ref[h + tm:2 * h + tm, :] = jnp.where(i < nt - 1, xn_ref[...] * scale + shift, 0.0)
        acc_ref[...] = jnp.zeros(acc_ref.shape, f32)

    ue_ref[...] = jnp.dot(xe_ref[...].astype(bf16), wu_ref[...], preferred_element_type=f32)
    v = jnp.dot(xe_ref[h:h + tm, :].astype(bf16), wv_ref[...], preferred_element_type=f32)
    p = FFN_CONV // 2
    u = cb_ref[...] + ue_ref[h - p:h - p + tm, :] * cw_ref[0:1, :]
    for j in range(1, FFN_CONV):
        u = u + ue_ref[h - p + j:h - p + j + tm, :] * cw_ref[j:j + 1, :]
    gated = (_gelu_exact(u) * v).astype(bf16)
    acc_ref[...] += jnp.dot(gated, wd_ref[...], preferred_element_type=f32)

    @pl.when(c == nf - 1)
    def _():
        o_ref[...] = _layernorm(alpha * x_ref[...] + gf_ref[...] * acc_ref[...], lw_ref[...], lb_ref[...])


def _ffn_call(x, sc, sh, gf, w_up, conv_w, conv_b, w_down, ln_w, ln_b, alpha):
    bsz, t, d = x.shape
    tm = min(512, t)
    nt = t // tm
    tf = D_FF // 2
    nf = D_FF // tf
    hb = tm // SUBLANES_V7X
    last_hb = t // SUBLANES_V7X - 1
    mod = pl.BlockSpec((None, 1, d), lambda b, i, c: (b, 0, 0))
    vec = pl.BlockSpec((1, d), lambda b, i, c: (0, 0))
    return pl.pallas_call(
        functools.partial(_ffn_kernel, tm=tm, nt=nt, nf=nf, alpha=alpha),
        grid=(bsz, nt, nf),
        in_specs=[
            pl.BlockSpec((None, tm, d), lambda b, i, c: (b, i, 0)),
            pl.BlockSpec((None, SUBLANES_V7X, d), lambda b, i, c: (b, jnp.maximum(i * hb - 1, 0), 0)),
            pl.BlockSpec((None, SUBLANES_V7X, d), lambda b, i, c: (b, jnp.minimum((i + 1) * hb, last_hb), 0)),
            mod, mod, mod,
            pl.BlockSpec((d, tf), lambda b, i, c: (0, c)),
            pl.BlockSpec((d, tf), lambda b, i, c: (0, nf + c)),
            pl.BlockSpec((FFN_CONV, tf), lambda b, i, c: (0, c)),
            pl.BlockSpec((1, tf), lambda b, i, c: (0, c)),
            pl.BlockSpec((tf, d), lambda b, i, c: (c, 0)),
            vec, vec,
        ],
        out_specs=pl.BlockSpec((None, tm, d), lambda b, i, c: (b, i, 0)),
        out_shape=jax.ShapeDtypeStruct((bsz, t, d), f32),
        scratch_shapes=[
            pltpu.VMEM((tm + 2 * SUBLANES_V7X, d), f32),
            pltpu.VMEM((tm + 2 * SUBLANES_V7X, tf), f32),
            pltpu.VMEM((tm, d), f32),
        ],
        compiler_params=_cparams(("parallel", "parallel", "arbitrary")),
        name="conv_ffn",
    )(x, x, x, sc, sh, gf, w_up, w_up, conv_w, conv_b.reshape(1, -1), w_down, ln_w.reshape(1, -1), ln_b.reshape(1, -1))


def _rot_cols(w, block):
    d, n = w.shape
    w4 = w.reshape(d, n // block, 2, block // 2)
    return jnp.concatenate([-w4[:, :, 1], w4[:, :, 0]], axis=-1).reshape(d, n)


def _ext_weight(w_in):
    d = w_in.shape[0]
    o = 0
    z = w_in[:, o:o + SSD_INNER]; o += SSD_INNER
    xbc = w_in[:, o:o + SSD_XBC]; o += SSD_XBC
    dt = w_in[:, o:o + 2 * SSD_HEADS]; o += 2 * SSD_HEADS
    dq = w_in[:, o:o + DIFF_QK]; o += DIFF_QK
    dk = w_in[:, o:o + DIFF_QK]; o += DIFF_QK
    dv = w_in[:, o:o + DIFF_WIDTH]; o += DIFF_WIDTH
    rq = w_in[:, o:o + RET_QK]; o += RET_QK
    rk = w_in[:, o:o + RET_QK]; o += RET_QK
    rv = w_in[:, o:o + RET_WIDTH]; o += RET_WIDTH
    rg = w_in[:, o:o + RET_WIDTH]; o += RET_WIDTH
    ax = DIFF_HEAD_DIM // 2
    dt_pad = jnp.zeros((d, LANES_V7X - SSD_HEADS), w_in.dtype)
    main = jnp.concatenate([z, xbc, dq, _rot_cols(dq, ax), dk, _rot_cols(dk, ax), dv,
                            rq, _rot_cols(rq, RET_QK_DIM), rk, _rot_cols(rk, RET_QK_DIM), rv, rg], axis=1)
    w_dt = jnp.concatenate([dt[:, :SSD_HEADS], dt_pad, dt[:, SSD_HEADS:], dt_pad], axis=1)
    return main.astype(bf16), w_dt.astype(bf16)


def _rope_tables(s, n_ctx):
    rows = s // GRID_W
    row = jnp.repeat(jnp.arange(rows, dtype=f32), GRID_W)
    col = jnp.tile(jnp.arange(GRID_W, dtype=f32), rows)
    n_ax = DIFF_HEAD_DIM // 4
    inv_ax = 1.0 / (ROPE_BASE ** (jnp.arange(n_ax, dtype=f32) / n_ax))
    ar = row[:, None] * inv_ax
    ac = col[:, None] * inv_ax
    ang = jnp.concatenate([ar, ar, ac, ac], axis=-1)
    ang = jnp.tile(ang, (1, 2))
    inv_ret = 1.0 / (ROPE_BASE ** jnp.linspace(0.0, 1.0, RET_QK_DIM // 2, dtype=f32))
    ang_c = jnp.arange(n_ctx, dtype=f32)[:, None] * inv_ret
    ang_l = (n_ctx + jnp.arange(s, dtype=f32))[:, None] * inv_ret
    tile_ret = lambda a: jnp.tile(jnp.concatenate([a, a], axis=-1), (1, RET_HEADS))
    ang = jnp.tile(ang, (1, DIFF_HEADS))
    return {
        "diff_l": (jnp.cos(ang), jnp.sin(ang)),
        "diff_c": (jnp.ones((n_ctx, DIFF_WIDTH), f32), jnp.zeros((n_ctx, DIFF_WIDTH), f32)),
        "ret_l": (jnp.cos(tile_ret(ang_l)), jnp.sin(tile_ret(ang_l))),
        "ret_c": (jnp.cos(tile_ret(ang_c)), jnp.sin(tile_ret(ang_c))),
    }


def kernel(x, c, ctx, c_ctx, w_ada, b_ada, w_in, ssd_conv_w, ssd_conv_b, ssd_a_log, ssd_dt_bias, ssd_d, ssd_norm_w,
           diff_lambda, diff_norm_w, ret_decay, ret_norm_w, w_out, ln1_w, ln1_b, ffn_w_up, ffn_conv_w, ffn_conv_b,
           ffn_w_down, ln2_w, ln2_b):
    bsz, s, d = x.shape
    n_ctx = ctx.shape[1]
    depth = w_ada.shape[0]
    alpha = (2.0 * depth) ** 0.25
    assert d == D_MODEL and s % 512 == 0 and n_ctx % ATT_TP == 0 and s % GRID_W == 0

    tabs = _rope_tables(s, n_ctx)
    nrow = -(-(bsz + 1) // SUBLANES_V7X) * SUBLANES_V7X
    cvecs = jnp.zeros((nrow, d), f32).at[:bsz].set(c).at[bsz].set(c_ctx)
    mod_all = _ada_call(cvecs, w_ada, b_ada)

    gw = SSD_HPG * SSD_HEAD_DIM
    zero_ssd = jnp.zeros((bsz, 2, SSD_GROUPS, SSD_STATE, gw), f32)
    zero_ret = jnp.zeros((bsz, 2, RET_HEADS // 2, 2 * RET_QK_DIM, RET_V_DIM), f32)

    xc = ctx
    for li in range(depth):
        last = li == depth - 1
        lam_init = 0.8 - 0.6 * math.exp(-0.3 * li)
        mod = mod_all[li]
        m_lat = [mod[:bsz, k * d:(k + 1) * d].reshape(bsz, 1, d) for k in range(6)]
        m_ctx = [jnp.broadcast_to(mod[bsz, k * d:(k + 1) * d].reshape(1, 1, d), (bsz, 1, d)) for k in range(6)]
        w_main, w_dt = _ext_weight(w_in[li])
        w_out_b = w_out[li].astype(bf16)
        w_up_b = ffn_w_up[li].astype(bf16)
        w_down_b = ffn_w_down[li].astype(bf16)
        d_skip = jnp.repeat(ssd_d[li], SSD_HEAD_DIM).reshape(1, SSD_INNER)

        proj = _modmm_call(x, m_lat[1], m_lat[0], w_main, bf16, 2)
        dt_l = _modmm_call(x, m_lat[1], m_lat[0], w_dt, f32, 1)
        proj_c = _modmm_call(xc, m_ctx[1], m_ctx[0], w_main, bf16, 2)
        dt_c = _modmm_call(xc, m_ctx[1], m_ctx[0], w_dt, f32, 1)

        xbc_c = _ssd_conv_call(proj_c, ssd_conv_w[li], ssd_conv_b[li])
        ys_c, hs = _ssd_scan_call(xbc_c, dt_c, ssd_a_log[li], ssd_dt_bias[li], zero_ssd)
        xbc = _ssd_conv_call(proj, ssd_conv_w[li], ssd_conv_b[li])
        ys, _ = _ssd_scan_call(xbc, dt_l, ssd_a_log[li], ssd_dt_bias[li], hs)

        yr_c, hr = _ret_scan_call(proj_c, *tabs["ret_c"], ret_decay[li], zero_ret)
        yr, _ = _ret_scan_call(proj, *tabs["ret_l"], ret_decay[li], hr)

        qt_c, k_c, vt_c = _attn_prep_call(proj_c, *tabs["diff_c"])
        qt_l, k_l, vt_l = _attn_prep_call(proj, *tabs["diff_l"])
        k_all = jnp.concatenate([k_l, k_c], axis=1)
        vt_all = jnp.concatenate([vt_l, vt_c], axis=3)
        y_diff = _attn_call(qt_l, k_all, vt_all, diff_lambda[li], diff_norm_w[li], lam_init)

        x = _mix_out_call(x, ys, xbc, proj, y_diff, yr, d_skip, ssd_norm_w[li], ret_norm_w[li], w_out_b,
                          m_lat[2], ln1_w[li], ln1_b[li], alpha)
        x = _ffn_call(x, m_lat[4], m_lat[3], m_lat[5], w_up_b, ffn_conv_w[li], ffn_conv_b[li], w_down_b,
                      ln2_w[li], ln2_b[li], alpha)
        if not last:
            yc_diff = _attn_call(qt_c, k_c, vt_c, diff_lambda[li], diff_norm_w[li], lam_init)
            xc = _mix_out_call(xc, ys_c, xbc_c, proj_c, yc_diff, yr_c, d_skip, ssd_norm_w[li], ret_norm_w[li], w_out_b,
                               m_ctx[2], ln1_w[li], ln1_b[li], alpha)
            xc = _ffn_call(xc, m_ctx[4], m_ctx[3], m_ctx[5], w_up_b, ffn_conv_w[li], ffn_conv_b[li], w_down_b,
                           ln2_w[li], ln2_b[li], alpha)
    return x
```

```python
import functools
import math

import jax
import jax.numpy as jnp
from jax import lax
from jax.experimental import pallas as pl
from jax.experimental.pallas import tpu as pltpu

f32 = jnp.float32
bf16 = jnp.bfloat16
HIGHEST = lax.Precision.HIGHEST

D_MODEL = 1024
GRID_W = 64
CHUNK = 128
ROPE_BASE = 10000.0
SSD_INNER = D_MODEL
SSD_HEAD_DIM = 64
SSD_HEADS = SSD_INNER // SSD_HEAD_DIM
SSD_GROUPS = 2
SSD_HPG = SSD_HEADS // SSD_GROUPS
SSD_STATE = 128
SSD_CONV = 5
SSD_XBC = SSD_INNER + 2 * SSD_GROUPS * SSD_STATE
SSD_COLS = SSD_INNER + SSD_XBC + 2 * SSD_HEADS
DIFF_WIDTH = D_MODEL // 2
DIFF_V_DIM = 128
DIFF_HEADS = DIFF_WIDTH // DIFF_V_DIM
DIFF_HEAD_DIM = DIFF_V_DIM // 2
DIFF_QK = DIFF_HEADS * 2 * DIFF_HEAD_DIM
DIFF_COLS = 2 * DIFF_QK + DIFF_WIDTH
RET_WIDTH = D_MODEL // 2
RET_V_DIM = 128
RET_HEADS = RET_WIDTH // RET_V_DIM
RET_QK_DIM = RET_V_DIM // 2
RET_QK = RET_HEADS * RET_QK_DIM
RET_COLS = 2 * RET_QK + 2 * RET_WIDTH
MIX_WIDTH = 2 * D_MODEL
D_FF = 11 * D_MODEL // 4
FFN_CONV = 3
EPS = 1e-5

LANES_V7X = 128
SUBLANES_V7X = 8
VMEM_LIMIT_V7X = 56 * 1024 * 1024

COL_Z = 0
COL_XBC = 8
COL_DQ = 20
COL_DK = 24
COL_DV = 28
COL_RQ = 32
COL_RK = 34
COL_RV = 36
COL_RG = 40
N_MAIN = 44 * LANES_V7X
NEG_BIG = -1e30


def _cparams(sem):
    return pltpu.CompilerParams(dimension_semantics=sem, vmem_limit_bytes=VMEM_LIMIT_V7X)


def _silu(x):
    return x * (1.0 / (1.0 + jnp.exp(-x)))


def _layernorm(x, w, b):
    mu = jnp.mean(x, axis=-1, keepdims=True)
    xc = x - mu
    var = jnp.mean(xc * xc, axis=-1, keepdims=True)
    return xc * lax.rsqrt(var + EPS) * w + b


def _rotate_half(x, half):
    n = x.shape[-1]
    lane = lax.broadcasted_iota(jnp.int32, x.shape, x.ndim - 1)
    first = (lane & (2 * half - 1)) < half
    return jnp.where(first, -pltpu.roll(x, n - half, x.ndim - 1), pltpu.roll(x, half, x.ndim - 1))


def _run_pipeline(step, n):
    step(0, 0, 1 if n > 1 else None, False)
    n_loop = max((n - 1) // 2 - 1, 0)

    def pair(k, carry):
        step(2 * k + 1, 1, 2 * k + 2, True)
        step(2 * k + 2, 0, 2 * k + 3, True)
        return carry

    lax.fori_loop(0, n_loop, pair, 0)
    for c in range(2 * n_loop + 1, n):
        step(c, c % 2, c + 1 if c + 1 < n else None, True)


def _split2(x):
    hi = x.astype(bf16)
    lo = (x - hi.astype(f32)).astype(bf16)
    return hi, lo


def _ada_kernel(c_ref, w_ref, b_ref, o_ref):
    c = c_ref[...]
    o_ref[...] = jnp.dot(_silu(c), w_ref[...], precision=HIGHEST, preferred_element_type=f32) + b_ref[...]


def _ada_call(cvecs, w_ada, b_ada):
    depth, d, n = w_ada.shape
    r = cvecs.shape[0]
    tn = 1536
    return pl.pallas_call(
        _ada_kernel,
        grid=(depth, n // tn),
        in_specs=[
            pl.BlockSpec((r, d), lambda l, j: (0, 0)),
            pl.BlockSpec((None, d, tn), lambda l, j: (l, 0, j)),
            pl.BlockSpec((None, 1, tn), lambda l, j: (l, 0, j)),
        ],
        out_specs=pl.BlockSpec((None, r, tn), lambda l, j: (l, 0, j)),
        out_shape=jax.ShapeDtypeStruct((depth, r, n), f32),
        compiler_params=_cparams(("parallel", "parallel")),
        name="ada",
    )(cvecs, w_ada, b_ada.reshape(depth, 1, n))


def _in_proj_kernel(x_ref, sc_ref, sh_ref, w_ref, wdt_ref, o_ref, odt_ref):
    xm = (x_ref[...] * (1.0 + sc_ref[...]) + sh_ref[...]).astype(bf16)
    half = w_ref.shape[1] // 2
    for j in range(2):
        cols = slice(j * half, (j + 1) * half)
        o_ref[:, cols] = jnp.dot(xm, w_ref[:, cols], preferred_element_type=f32).astype(o_ref.dtype)
    odt_ref[...] = jnp.dot(xm, wdt_ref[...], preferred_element_type=f32)


def _in_proj_call(x, sc, sh, w_main, w_dt):
    bsz, t, d = x.shape
    n = w_main.shape[1]
    ndt = w_dt.shape[1]
    tm = min(512, t)
    mod = pl.BlockSpec((None, 1, d), lambda b, i: (b, 0, 0))
    return pl.pallas_call(
        _in_proj_kernel,
        grid=(bsz, t // tm),
        in_specs=[
            pl.BlockSpec((None, tm, d), lambda b, i: (b, i, 0)),
            mod, mod,
            pl.BlockSpec((d, n), lambda b, i: (0, 0)),
            pl.BlockSpec((d, ndt), lambda b, i: (0, 0)),
        ],
        out_specs=[
            pl.BlockSpec((None, tm, n), lambda b, i: (b, i, 0)),
            pl.BlockSpec((None, tm, ndt), lambda b, i: (b, i, 0)),
        ],
        out_shape=[
            jax.ShapeDtypeStruct((bsz, t, n), bf16),
            jax.ShapeDtypeStruct((bsz, t, ndt), f32),
        ],
        compiler_params=_cparams(("parallel", "parallel")),
        name="in_proj",
    )(x, sc, sh, w_main, w_dt)


def _ssd_conv_kernel(cur_ref, prev_ref, next_ref, w_ref, b_ref, o_ref, ext_ref, *, tm, nt):
    i = pl.program_id(1)
    h = SUBLANES_V7X
    ext_ref[0:h, :] = jnp.where(i > 0, prev_ref[...].astype(f32)[h:2 * h], 0.0)
    ext_ref[h:h + tm, :] = cur_ref[...].astype(f32)
    ext_ref[h + tm:2 * h + tm, :] = jnp.where(i < nt - 1, next_ref[...].astype(f32)[0:h], 0.0)
    p = SSD_CONV // 2
    acc = ext_ref[h - p:h - p + tm, :] * w_ref[0:1, :]
    for j in range(1, SSD_CONV):
        acc = acc + ext_ref[h - p + j:h - p + j + tm, :] * w_ref[j:j + 1, :]
    o_ref[...] = _silu(acc + b_ref[...]).astype(o_ref.dtype)


def _ssd_conv_call(proj, conv_w, conv_b):
    bsz, t, _ = proj.shape
    tm = min(512, t)
    nt = t // tm
    cw = 512
    nc = SSD_XBC // cw
    c0 = COL_XBC * LANES_V7X // cw
    halo = 2 * SUBLANES_V7X
    hb = tm // halo
    last_hb = t // halo - 1
    return pl.pallas_call(
        functools.partial(_ssd_conv_kernel, tm=tm, nt=nt),
        grid=(bsz, nt, nc),
        in_specs=[
            pl.BlockSpec((None, tm, cw), lambda b, i, c: (b, i, c0 + c)),
            pl.BlockSpec((None, halo, cw), lambda b, i, c: (b, jnp.maximum(i * hb - 1, 0), c0 + c)),
            pl.BlockSpec((None, halo, cw), lambda b, i, c: (b, jnp.minimum((i + 1) * hb, last_hb), c0 + c)),
            pl.BlockSpec((SSD_CONV, cw), lambda b, i, c: (0, c)),
            pl.BlockSpec((1, cw), lambda b, i, c: (0, c)),
        ],
        out_specs=pl.BlockSpec((None, tm, cw), lambda b, i, c: (b, i, c)),
        out_shape=jax.ShapeDtypeStruct((bsz, t, SSD_XBC), bf16),
        scratch_shapes=[pltpu.VMEM((tm + 2 * SUBLANES_V7X, cw), f32)],
        compiler_params=_cparams(("parallel", "parallel", "parallel")),
        name="ssd_conv",
    )(proj, proj, proj, conv_w, conv_b.reshape(1, SSD_XBC))


def _ssd_scan_kernel(cq_ref, bk_ref, x_ref, dt_ref, alog_ref, dtb_ref, tri_ref, e_ref, h0_ref,
                     y_ref, hout_ref, h_ref, *, ns, cps):
    d = pl.program_id(1)
    j = pl.program_id(2)
    L = CHUNK
    gw = SSD_HPG * SSD_HEAD_DIM

    @pl.when(j == 0)
    def _():
        h_ref[...] = h0_ref[...]

    tri = tri_ref[...]
    mask = tri > 0.5
    a_neg = -jnp.exp(alog_ref[...])
    lane = lax.broadcasted_iota(jnp.int32, (L, LANES_V7X), 1)
    lo_half = lane < SSD_HEAD_DIM

    def one_chunk(rows):
        sp = jax.nn.softplus(dt_ref[rows, :] + dtb_ref[...])
        la = sp * a_neg
        cum = jnp.dot(tri, la, precision=HIGHEST, preferred_element_type=f32)
        cum_t = cum.T
        sp_t = sp.T
        tot = jnp.sum(la, axis=0, keepdims=True)
        w_end = jnp.exp(tot - cum) * sp
        e_cum = jnp.exp(cum)
        dec = jnp.broadcast_to(jnp.exp(tot), (SUBLANES_V7X, LANES_V7X))
        pieces = _split2(w_end) + _split2(e_cum) + _split2(dec)
        r = jnp.dot(jnp.concatenate(pieces, axis=0), e_ref[...], preferred_element_type=f32)
        w_x = r[0:L] + r[L:2 * L]
        ec_x = r[2 * L:3 * L] + r[3 * L:4 * L]
        dec_x = r[4 * L:4 * L + 1] + r[4 * L + SUBLANES_V7X:4 * L + SUBLANES_V7X + 1]

        x = x_ref[rows, :].astype(f32)
        xw = (x * w_x).astype(bf16)
        for g in range(SSD_GROUPS):
            qg = cq_ref[rows, g * SSD_STATE:(g + 1) * SSD_STATE]
            kg = bk_ref[rows, g * SSD_STATE:(g + 1) * SSD_STATE]
            scores = lax.dot_general(qg, kg, (((1,), (1,)), ((), ())), preferred_element_type=f32)
            kg_t = kg.astype(f32).T.astype(bf16)
            h_old = h_ref[g]
            y_off = jnp.dot(qg, h_old.astype(bf16), preferred_element_type=f32) * ec_x[:, g * gw:(g + 1) * gw]
            h_ref[g] = dec_x[:, g * gw:(g + 1) * gw] * h_old + jnp.dot(
                kg_t, xw[:, g * gw:(g + 1) * gw], preferred_element_type=f32)
            outs = []
            for pp in range(SSD_HPG // 2):
                ms = []
                for hh in range(2):
                    head = g * SSD_HPG + 2 * pp + hh
                    seg = cum[:, head:head + 1] - cum_t[head:head + 1, :]
                    decay = jnp.exp(jnp.where(mask, seg, NEG_BIG))
                    ms.append((scores * decay * sp_t[head:head + 1, :]).astype(bf16))
                c_lo = g * gw + pp * LANES_V7X
                xp = x[:, c_lo:c_lo + LANES_V7X]
                rhs = jnp.concatenate([jnp.where(lo_half, xp, 0.0), jnp.where(lo_half, 0.0, xp)],
                                      axis=0).astype(bf16)
                outs.append(jnp.dot(jnp.concatenate(ms, axis=1), rhs, preferred_element_type=f32))
            y_ref[rows, g * gw:(g + 1) * gw] = (jnp.concatenate(outs, axis=1) + y_off).astype(y_ref.dtype)

    for t in range(cps):
        off = (t + d * (cps - 1 - 2 * t)) * L
        one_chunk(pl.ds(pl.multiple_of(off, L), L))

    @pl.when(j == ns - 1)
    def _():
        hout_ref[...] = h_ref[...]


def _chunks_per_step(nc):
    return 4 if nc % 4 == 0 else (2 if nc % 2 == 0 else 1)


def _ssd_scan_call(xbc, dt_raw, a_log, dt_bias, h0):
    bsz, t, _ = xbc.shape
    cps = _chunks_per_step(t // CHUNK)
    L = CHUNK
    rows = cps * L
    ns = t // rows
    gw = SSD_HPG * SSD_HEAD_DIM

    def chunk(d, j):
        return j + d * (ns - 1 - 2 * j)

    alog = jnp.zeros((2, 1, LANES_V7X), f32).at[:, 0, :SSD_HEADS].set(a_log)
    dtb = jnp.zeros((2, 1, LANES_V7X), f32).at[:, 0, :SSD_HEADS].set(dt_bias)
    idx = jnp.arange(L)
    tri = jnp.stack([(idx[None, :] <= idx[:, None]), (idx[None, :] >= idx[:, None])]).astype(f32)
    head_col = jnp.arange(LANES_V7X)[:, None]
    heads = (jnp.arange(SSD_INNER) // SSD_HEAD_DIM)[None, :]
    expand = (head_col == heads).astype(bf16)
    bcol = SSD_INNER // (SSD_GROUPS * SSD_STATE)
    return pl.pallas_call(
        functools.partial(_ssd_scan_kernel, ns=ns, cps=cps),
        grid=(bsz, 2, ns),
        in_specs=[
            pl.BlockSpec((None, rows, SSD_GROUPS * SSD_STATE), lambda b, d, j: (b, chunk(d, j), bcol + 1)),
            pl.BlockSpec((None, rows, SSD_GROUPS * SSD_STATE), lambda b, d, j: (b, chunk(d, j), bcol)),
            pl.BlockSpec((None, rows, SSD_INNER), lambda b, d, j: (b, chunk(d, j), 0)),
            pl.BlockSpec((None, rows, LANES_V7X), lambda b, d, j: (b, chunk(d, j), d)),
            pl.BlockSpec((None, 1, LANES_V7X), lambda b, d, j: (d, 0, 0)),
            pl.BlockSpec((None, 1, LANES_V7X), lambda b, d, j: (d, 0, 0)),
            pl.BlockSpec((None, L, L), lambda b, d, j: (d, 0, 0)),
            pl.BlockSpec((LANES_V7X, SSD_INNER), lambda b, d, j: (0, 0)),
            pl.BlockSpec((None, None, SSD_GROUPS, SSD_STATE, gw), lambda b, d, j: (b, d, 0, 0, 0)),
        ],
        out_specs=[
            pl.BlockSpec((None, None, rows, SSD_INNER), lambda b, d, j: (b, d, chunk(d, j), 0)),
            pl.BlockSpec((None, None, SSD_GROUPS, SSD_STATE, gw), lambda b, d, j: (b, d, 0, 0, 0)),
        ],
        out_shape=[
            jax.ShapeDtypeStruct((bsz, 2, t, SSD_INNER), bf16),
            jax.ShapeDtypeStruct((bsz, 2, SSD_GROUPS, SSD_STATE, gw), f32),
        ],
        scratch_shapes=[pltpu.VMEM((SSD_GROUPS, SSD_STATE, gw), f32)],
        compiler_params=_cparams(("parallel", "parallel", "arbitrary")),
        name="ssd_scan",
    )(xbc, xbc, xbc, dt_raw, alog, dtb, tri, expand, h0)


def _ret_scan_kernel(q_ref, k_ref, v_ref, cos_ref, sin_ref, rd_ref, h0_ref,
                     y_ref, hout_ref, h_ref, dec_ref, ecx_ref, tex_ref, dst_ref, *, ns, cps):
    d = pl.program_id(1)
    j = pl.program_id(2)
    L = CHUNK
    npair = RET_HEADS // 2

    @pl.when(j == 0)
    def _():
        h_ref[...] = h0_ref[...]
        lg = -jnp.exp(rd_ref[...])
        li = lax.broadcasted_iota(jnp.int32, (L, L), 0)
        si = lax.broadcasted_iota(jnp.int32, (L, L), 1)
        dist = jnp.where(d == 0, li - si, si - li)
        causal = dist >= 0
        distf = dist.astype(f32)
        lane = lax.broadcasted_iota(jnp.int32, (1, RET_WIDTH), 1)
        lgx = jnp.zeros((1, RET_WIDTH), f32)
        for h in range(RET_HEADS):
            lgh = lg[:, h:h + 1]
            dec_ref[h] = jnp.exp(jnp.where(causal, distf * lgh, NEG_BIG))
            lgx = jnp.where((lane >= h * RET_V_DIM) & (lane < (h + 1) * RET_V_DIM), lgh, lgx)
        pos = lax.broadcasted_iota(jnp.int32, (L, 1), 0)
        steps = jnp.where(d == 0, pos + 1, L - pos).astype(f32)
        cumx = steps * lgx
        totx = float(L) * lgx
        ecx_ref[...] = jnp.exp(cumx)
        tex_ref[...] = jnp.exp(totx - cumx)
        rowi = lax.broadcasted_iota(jnp.int32, (2 * RET_QK_DIM, RET_V_DIM), 0)
        for pp in range(npair):
            d0 = jnp.exp(float(L) * lg[:, 2 * pp:2 * pp + 1])
            d1 = jnp.exp(float(L) * lg[:, 2 * pp + 1:2 * pp + 2])
            dst_ref[pp] = jnp.where(rowi < RET_QK_DIM, d0, d1)

    ecx = ecx_ref[...]
    tex = tex_ref[...]
    hi_half = lax.broadcasted_iota(jnp.int32, (L, LANES_V7X), 1) >= RET_QK_DIM
    zeros_v = jnp.zeros((L, RET_V_DIM), bf16)

    def one_chunk(rows):
        cos = cos_ref[rows, :]
        sin = sin_ref[rows, :]
        q = q_ref[rows, :].astype(f32)
        k = k_ref[rows, :].astype(f32)
        q = q * cos + _rotate_half(q, RET_QK_DIM // 2) * sin
        k = (k * cos + _rotate_half(k, RET_QK_DIM // 2) * sin) * (RET_QK_DIM ** -0.5)
        vb = v_ref[rows, :]
        vw = (vb.astype(f32) * tex).astype(bf16)
        for pp in range(npair):
            qp = q[:, pp * LANES_V7X:(pp + 1) * LANES_V7X]
            kp = k[:, pp * LANES_V7X:(pp + 1) * LANES_V7X]
            kp_b = kp.astype(bf16)
            kp_t = kp.T.astype(bf16)
            r_old = h_ref[pp]
            r_b = r_old.astype(bf16)
            ms, offs = [], []
            for hh in range(2):
                h = 2 * pp + hh
                qm = (jnp.where(hi_half, qp, 0.0) if hh else jnp.where(hi_half, 0.0, qp)).astype(bf16)
                scores = lax.dot_general(qm, kp_b, (((1,), (1,)), ((), ())), preferred_element_type=f32)
                ms.append((scores * dec_ref[h]).astype(bf16))
                offs.append(jnp.dot(qm, r_b, preferred_element_type=f32))
            va = vb[:, (2 * pp) * RET_V_DIM:(2 * pp + 1) * RET_V_DIM]
            vc = vb[:, (2 * pp + 1) * RET_V_DIM:(2 * pp + 2) * RET_V_DIM]
            rhs = jnp.concatenate([jnp.concatenate([va, zeros_v], axis=1),
                                   jnp.concatenate([zeros_v, vc], axis=1)], axis=0)
            y_diag = jnp.dot(jnp.concatenate(ms, axis=1), rhs, preferred_element_type=f32)
            c0 = 2 * pp * RET_V_DIM
            y_ref[rows, c0:c0 + 2 * RET_V_DIM] = (
                y_diag + jnp.concatenate(offs, axis=1) * ecx[:, c0:c0 + 2 * RET_V_DIM]).astype(y_ref.dtype)
            upd = jnp.dot(kp_t, vw[:, c0:c0 + 2 * RET_V_DIM], preferred_element_type=f32)
            new = jnp.concatenate([upd[0:RET_QK_DIM, 0:RET_V_DIM],
                                   upd[RET_QK_DIM:2 * RET_QK_DIM, RET_V_DIM:2 * RET_V_DIM]], axis=0)
            h_ref[pp] = dst_ref[pp] * r_old + new

    for t in range(cps):
        off = (t + d * (cps - 1 - 2 * t)) * L
        one_chunk(pl.ds(pl.multiple_of(off, L), L))

    @pl.when(j == ns - 1)
    def _():
        hout_ref[...] = h_ref[...]


def _ret_scan_call(proj, cos, sin, decay_raw, h0):
    bsz, t, _ = proj.shape
    cps = _chunks_per_step(t // CHUNK)
    L = CHUNK
    rows = cps * L
    ns = t // rows
    npair = RET_HEADS // 2

    def chunk(d, j):
        return j + d * (ns - 1 - 2 * j)

    rd = jnp.zeros((2, 1, LANES_V7X), f32).at[:, 0, :RET_HEADS].set(decay_raw)
    qk = lambda c: pl.BlockSpec((None, rows, RET_QK), lambda b, d, j: (b, chunk(d, j), c // 2))
    tab = pl.BlockSpec((rows, RET_QK), lambda b, d, j: (chunk(d, j), 0))
    st = pl.BlockSpec((None, None, npair, 2 * RET_QK_DIM, RET_V_DIM), lambda b, d, j: (b, d, 0, 0, 0))
    return pl.pallas_call(
        functools.partial(_ret_scan_kernel, ns=ns, cps=cps),
        grid=(bsz, 2, ns),
        in_specs=[
            qk(COL_RQ), qk(COL_RK),
            pl.BlockSpec((None, rows, RET_WIDTH), lambda b, d, j: (b, chunk(d, j), COL_RV // 4)),
            tab, tab,
            pl.BlockSpec((None, 1, LANES_V7X), lambda b, d, j: (d, 0, 0)),
            st,
        ],
        out_specs=[
            pl.BlockSpec((None, None, rows, RET_WIDTH), lambda b, d, j: (b, d, chunk(d, j), 0)),
            st,
        ],
        out_shape=[
            jax.ShapeDtypeStruct((bsz, 2, t, RET_WIDTH), bf16),
            jax.ShapeDtypeStruct((bsz, 2, npair, 2 * RET_QK_DIM, RET_V_DIM), f32),
        ],
        scratch_shapes=[
            pltpu.VMEM((npair, 2 * RET_QK_DIM, RET_V_DIM), f32),
            pltpu.VMEM((RET_HEADS, L, L), f32),
            pltpu.VMEM((L, RET_WIDTH), f32),
            pltpu.VMEM((L, RET_WIDTH), f32),
            pltpu.VMEM((npair, 2 * RET_QK_DIM, RET_V_DIM), f32),
        ],
        compiler_params=_cparams(("parallel", "parallel", "arbitrary")),
        name="ret_scan",
    )(proj, proj, proj, cos, sin, rd, h0)


ATT_TP = 256
ATT_TQS = 256
ATT_VROWS = DIFF_V_DIM + 16
LOG2E = 1.4426950408889634


def _attn_prep_kernel(q_ref, k_ref, v_ref, cos_ref, sin_ref, qt_ref, ko_ref, vt_ref):
    cos = cos_ref[...]
    sin = sin_ref[...]
    ax = DIFF_HEAD_DIM // 4
    q = q_ref[...].astype(f32)
    k = k_ref[...].astype(f32)
    q = (q * cos + _rotate_half(q, ax) * sin) * (DIFF_HEAD_DIM ** -0.5 * LOG2E)
    ko_ref[...] = (k * cos + _rotate_half(k, ax) * sin).astype(bf16)
    v = v_ref[...].astype(f32)
    ones = jnp.ones((ATT_VROWS - DIFF_V_DIM, vt_ref.shape[2]), bf16)
    for h in range(DIFF_HEADS):
        cols = slice(h * LANES_V7X, (h + 1) * LANES_V7X)
        qt_ref[h] = q[:, cols].T.astype(bf16)
        vt_ref[h, 0:DIFF_V_DIM, :] = v[:, cols].T.astype(bf16)
        vt_ref[h, DIFF_V_DIM:ATT_VROWS, :] = ones


def _attn_prep_call(proj, cos, sin):
    bsz, t, _ = proj.shape
    tp = ATT_TP
    nt = t // tp
    w = DIFF_WIDTH
    col = lambda c0: pl.BlockSpec((None, tp, w), lambda b, i: (b, i, c0 * LANES_V7X // w))
    tab = pl.BlockSpec((tp, w), lambda b, i: (i, 0))
    return pl.pallas_call(
        _attn_prep_kernel,
        grid=(bsz, nt),
        in_specs=[col(COL_DQ), col(COL_DK), col(COL_DV), tab, tab],
        out_specs=[
            pl.BlockSpec((None, DIFF_HEADS, LANES_V7X, tp), lambda b, i: (b, 0, 0, i)),
            pl.BlockSpec((None, tp, w), lambda b, i: (b, i, 0)),
            pl.BlockSpec((None, DIFF_HEADS, ATT_VROWS, tp), lambda b, i: (b, 0, 0, i)),
        ],
        out_shape=[
            jax.ShapeDtypeStruct((bsz, DIFF_HEADS, 2 * DIFF_HEAD_DIM, t), bf16),
            jax.ShapeDtypeStruct((bsz, t, DIFF_WIDTH), bf16),
            jax.ShapeDtypeStruct((bsz, DIFF_HEADS, ATT_VROWS, t), bf16),
        ],
        compiler_params=_cparams(("parallel", "parallel")),
        name="attn_prep",
    )(proj, proj, proj, cos, sin)


def _attn_kernel(qt_ref, k_ref, vt_ref, lam_ref, nw_ref, o_ref, q2_ref, m_ref, acc_ref, s_ref, p_ref, a_ref, mx_ref,
                 *, nk, tk, lam_init):
    tq = qt_ref.shape[1]
    qt = qt_ref[...]
    row = lax.broadcasted_iota(jnp.int32, qt.shape, 0)
    q2_ref[0] = jnp.where(row < DIFF_HEAD_DIM, qt, jnp.zeros_like(qt))
    q2_ref[1] = jnp.where(row < DIFF_HEAD_DIM, jnp.zeros_like(qt), qt)
    m_ref[...] = jnp.full(m_ref.shape, NEG_BIG, f32)
    acc_ref[...] = jnp.zeros(acc_ref.shape, f32)
    chains = [(mi, slice(sb * ATT_TQS, (sb + 1) * ATT_TQS)) for mi in range(2) for sb in range(tq // ATT_TQS)]

    def key_chunk(c):
        return k_ref[pl.ds(pl.multiple_of(c * tk, ATT_TP), tk), :]

    def value_chunk(c):
        return vt_ref[:, pl.ds(pl.multiple_of(c * tk, ATT_TP), tk)]

    def step(c, par, c_next, has_prev, softmax=True):
        oth = 1 - par
        kn = None if c_next is None else key_chunk(c_next)
        vt = value_chunk(c - 1) if has_prev else None
        for mi, cols in chains:
            if kn is not None:
                scores(kn, oth, mi, cols)
            if has_prev:
                acc_ref[mi, :, cols] = (a_ref[oth, mi, :, cols] * acc_ref[mi, :, cols]
                                        + jnp.dot(vt, p_ref[oth, mi, :, cols], preferred_element_type=f32))
            if softmax:
                m_old = m_ref[mi, :, cols]
                m_new = jnp.maximum(m_old, mx_ref[par, mi, :, cols])
                p_ref[par, mi, :, cols] = jnp.exp2(s_ref[par, mi, :, cols] - m_new).astype(bf16)
                a_ref[par, mi, :, cols] = jnp.exp2(m_old - m_new)
                m_ref[mi, :, cols] = m_new

    def scores(kc, slot, mi, cols):
        s = jnp.dot(kc, q2_ref[mi, :, cols], preferred_element_type=f32)
        s_ref[slot, mi, :, cols] = s
        mx_ref[slot, mi, :, cols] = jnp.max(s, axis=0, keepdims=True)

    k0 = key_chunk(0)
    for mi, cols in chains:
        scores(k0, 0, mi, cols)
    _run_pipeline(step, nk)
    step(nk, nk % 2, None, True, softmax=False)

    lp = lam_ref[...]
    lam = (jnp.exp(jnp.sum(lp[0:1] * lp[1:2], axis=1, keepdims=True))
           - jnp.exp(jnp.sum(lp[2:3] * lp[3:4], axis=1, keepdims=True)) + lam_init)
    nv = DIFF_V_DIM
    ot = (acc_ref[0, 0:nv, :] / acc_ref[0, nv:nv + 1, :]
          - lam * (acc_ref[1, 0:nv, :] / acc_ref[1, nv:nv + 1, :]))
    o = ot.T
    ms = jnp.mean(o * o, axis=-1, keepdims=True)
    o_ref[...] = (o * lax.rsqrt(ms + EPS) * nw_ref[...] * (1.0 - lam_init)).astype(o_ref.dtype)


def _attn_call(qt, k_all, vt_all, lam_p, norm_w, lam_init):
    bsz, nh, _, sq = qt.shape
    sk = k_all.shape[1]
    tk = 3 * ATT_TP if sk % (3 * ATT_TP) == 0 else ATT_TP
    nk = sk // tk
    tq = min(1024, sq)
    return pl.pallas_call(
        functools.partial(_attn_kernel, nk=nk, tk=tk, lam_init=lam_init),
        grid=(bsz, nh, sq // tq),
        in_specs=[
            pl.BlockSpec((None, None, 2 * DIFF_HEAD_DIM, tq), lambda b, h, i: (b, h, 0, i)),
            pl.BlockSpec((None, sk, LANES_V7X), lambda b, h, i: (b, 0, h)),
            pl.BlockSpec((None, None, ATT_VROWS, sk), lambda b, h, i: (b, h, 0, 0)),
            pl.BlockSpec((4, DIFF_HEAD_DIM), lambda b, h, i: (0, 0)),
            pl.BlockSpec((1, DIFF_V_DIM), lambda b, h, i: (0, 0)),
        ],
        out_specs=pl.BlockSpec((None, tq, DIFF_V_DIM), lambda b, h, i: (b, i, h)),
        out_shape=jax.ShapeDtypeStruct((bsz, sq, DIFF_WIDTH), bf16),
        scratch_shapes=[
            pltpu.VMEM((2, 2 * DIFF_HEAD_DIM, tq), bf16),
            pltpu.VMEM((2, 1, tq), f32),
            pltpu.VMEM((2, ATT_VROWS, tq), f32),
            pltpu.VMEM((2, 2, tk, tq), f32),
            pltpu.VMEM((2, 2, tk, tq), bf16),
            pltpu.VMEM((2, 2, 1, tq), f32),
            pltpu.VMEM((2, 2, 1, tq), f32),
        ],
        compiler_params=_cparams(("parallel", "parallel", "parallel")),
        name="diff_attn",
    )(qt, k_all, vt_all, lam_p, norm_w.reshape(1, DIFF_V_DIM))


def _mix_out_kernel(x_ref, ysf_ref, ysb_ref, xs_ref, z_ref, yd_ref, yrf_ref, yrb_ref, g_ref,
                    dsk_ref, snw_ref, rnw_ref, wo_ref, ga_ref, lw_ref, lb_ref, o_ref, *, alpha):
    y = ysf_ref[...].astype(f32) + ysb_ref[...].astype(f32) + xs_ref[...].astype(f32) * dsk_ref[...]
    y = y * _silu(z_ref[...].astype(f32))
    y_ssd = y * lax.rsqrt(jnp.mean(y * y, axis=-1, keepdims=True) + EPS) * snw_ref[...]
    yr = yrf_ref[...].astype(f32) + yrb_ref[...].astype(f32)
    gate = _silu(g_ref[...].astype(f32))
    rets = []
    for h in range(RET_HEADS):
        yh = yr[:, h * RET_V_DIM:(h + 1) * RET_V_DIM]
        mu = jnp.mean(yh, axis=-1, keepdims=True)
        yc = yh - mu
        var = jnp.mean(yc * yc, axis=-1, keepdims=True)
        rets.append(yc * lax.rsqrt(var + EPS) * rnw_ref[...])
    y_ret = jnp.concatenate(rets, axis=1) * gate
    ycat = jnp.concatenate([y_ssd.astype(bf16), yd_ref[...], y_ret.astype(bf16)], axis=1)
    mixed = jnp.dot(ycat, wo_ref[...], preferred_element_type=f32)
    o_ref[...] = _layernorm(alpha * x_ref[...] + ga_ref[...] * mixed, lw_ref[...], lb_ref[...])


def _mix_out_call(x, y_ssd, xbc, proj, y_diff, y_ret, d_skip, ssd_norm_w, ret_norm_w, w_out, g_a, ln_w, ln_b, alpha):
    bsz, t, d = x.shape
    tm = min(512, t)
    row = lambda width, c: pl.BlockSpec((None, tm, width), lambda b, i: (b, i, c))
    two = lambda width, dd: pl.BlockSpec((None, None, tm, width), lambda b, i: (b, dd, i, 0))
    vec = lambda width: pl.BlockSpec((1, width), lambda b, i: (0, 0))
    return pl.pallas_call(
        functools.partial(_mix_out_kernel, alpha=alpha),
        grid=(bsz, t // tm),
        in_specs=[
            row(d, 0),
            two(SSD_INNER, 0), two(SSD_INNER, 1),
            row(SSD_INNER, 0),
            row(SSD_INNER, COL_Z * LANES_V7X // SSD_INNER),
            row(DIFF_WIDTH, 0),
            two(RET_WIDTH, 0), two(RET_WIDTH, 1),
            row(RET_WIDTH, COL_RG * LANES_V7X // RET_WIDTH),
            vec(SSD_INNER), vec(SSD_INNER), vec(RET_V_DIM),
            pl.BlockSpec((MIX_WIDTH, d), lambda b, i: (0, 0)),
            pl.BlockSpec((None, 1, d), lambda b, i: (b, 0, 0)),
            vec(d), vec(d),
        ],
        out_specs=row(d, 0),
        out_shape=jax.ShapeDtypeStruct((bsz, t, d), f32),
        compiler_params=_cparams(("parallel", "parallel")),
        name="mix_out",
    )(x, y_ssd, y_ssd, xbc, proj, y_diff, y_ret, y_ret, proj,
      d_skip, ssd_norm_w.reshape(1, -1), ret_norm_w.reshape(1, -1), w_out, g_a, ln_w.reshape(1, -1), ln_b.reshape(1, -1))


def _gelu_exact(x):
    return 0.5 * x * (1.0 + lax.erf(x * (2.0 ** -0.5)))


FFN_SPLIT = 1


def _ffn_kernel(x_ref, xp_ref, xn_ref, sc_ref, sh_ref, gf_ref, wu_ref, wv_ref, cw_ref, cb_ref, wd_ref,
                lw_ref, lb_ref, o_ref, xe_ref, ue_ref, acc_ref, *, tm, nt, nf, alpha):
    i = pl.program_id(1)
    c = pl.program_id(2)
    h = SUBLANES_V7X
    hm = tm // FFN_SPLIT

    @pl.when(c == 0)
    def _():
        scale = 1.0 + sc_ref[...]
        shift = sh_ref[...]
        xe_ref[0:h, :] = jnp.where(i > 0, xp_ref[...] * scale + shift, 0.0)
        xe_ref[h:h + tm, :] = x_ref[...] * scale + shift
        xe_ref[h + tm:2 * h + tm, :] = jnp.where(i < nt - 1, xn_ref[...] * scale + shift, 0.0)
        acc_ref[...] = jnp.zeros(acc_ref.shape, f32)

    vs = []
    for r in range(FFN_SPLIT):
        r0 = r * hm
        ue_ref[r] = jnp.dot(xe_ref[r0:r0 + hm + 2 * h, :].astype(bf16), wu_ref[...], preferred_element_type=f32)
        vs.append(jnp.dot(xe_ref[h + r0:h + r0 + hm, :].astype(bf16), wv_ref[...], preferred_element_type=f32))
    p = FFN_CONV // 2
    for r in range(FFN_SPLIT):
        u = cb_ref[...] + ue_ref[r, h - p:h - p + hm, :] * cw_ref[0:1, :]
        for j in range(1, FFN_CONV):
            u = u + ue_ref[r, h - p + j:h - p + j + hm, :] * cw_ref[j:j + 1, :]
        gated = (_gelu_exact(u) * vs[r]).astype(bf16)
        acc_ref[r * hm:(r + 1) * hm, :] += jnp.dot(gated, wd_ref[...], preferred_element_type=f32)

    @pl.when(c == nf - 1)
    def _():
        o_ref[...] = _layernorm(alpha * x_ref[...] + gf_ref[...] * acc_ref[...], lw_ref[...], lb_ref[...])


def _ffn_call(x, sc, sh, gf, w_up, conv_w, conv_b, w_down, ln_w, ln_b, alpha):
    bsz, t, d = x.shape
    tm = min(512, t)
    nt = t // tm
    tf = D_FF // 2
    nf = D_FF // tf
    hb = tm // SUBLANES_V7X
    last_hb = t // SUBLANES_V7X - 1
    mod = pl.BlockSpec((None, 1, d), lambda b, i, c: (b, 0, 0))
    vec = pl.BlockSpec((1, d), lambda b, i, c: (0, 0))
    return pl.pallas_call(
        functools.partial(_ffn_kernel, tm=tm, nt=nt, nf=nf, alpha=alpha),
        grid=(bsz, nt, nf),
        in_specs=[
            pl.BlockSpec((None, tm, d), lambda b, i, c: (b, i, 0)),
            pl.BlockSpec((None, SUBLANES_V7X, d), lambda b, i, c: (b, jnp.maximum(i * hb - 1, 0), 0)),
            pl.BlockSpec((None, SUBLANES_V7X, d), lambda b, i, c: (b, jnp.minimum((i + 1) * hb, last_hb), 0)),
            mod, mod, mod,
            pl.BlockSpec((d, tf), lambda b, i, c: (0, c)),
            pl.BlockSpec((d, tf), lambda b, i, c: (0, nf + c)),
            pl.BlockSpec((FFN_CONV, tf), lambda b, i, c: (0, c)),
            pl.BlockSpec((1, tf), lambda b, i, c: (0, c)),
            pl.BlockSpec((tf, d), lambda b, i, c: (c, 0)),
            vec, vec,
        ],
        out_specs=pl.BlockSpec((None, tm, d), lambda b, i, c: (b, i, 0)),
        out_shape=jax.ShapeDtypeStruct((bsz, t, d), f32),
        scratch_shapes=[
            pltpu.VMEM((tm + 2 * SUBLANES_V7X, d), f32),
            pltpu.VMEM((FFN_SPLIT, tm // FFN_SPLIT + 2 * SUBLANES_V7X, tf), f32),
            pltpu.VMEM((tm, d), f32),
        ],
        compiler_params=_cparams(("parallel", "parallel", "arbitrary")),
        name="conv_ffn",
    )(x, x, x, sc, sh, gf, w_up, w_up, conv_w, conv_b.reshape(1, -1), w_down, ln_w.reshape(1, -1), ln_b.reshape(1, -1))


def _ext_weight(w_in):
    d = w_in.shape[0]
    o = 0
    z = w_in[:, o:o + SSD_INNER]; o += SSD_INNER
    xbc = w_in[:, o:o + SSD_XBC]; o += SSD_XBC
    dt = w_in[:, o:o + 2 * SSD_HEADS]; o += 2 * SSD_HEADS
    dq = w_in[:, o:o + DIFF_QK]; o += DIFF_QK
    dk = w_in[:, o:o + DIFF_QK]; o += DIFF_QK
    dv = w_in[:, o:o + DIFF_WIDTH]; o += DIFF_WIDTH
    rq = w_in[:, o:o + RET_QK]; o += RET_QK
    rk = w_in[:, o:o + RET_QK]; o += RET_QK
    rv = w_in[:, o:o + RET_WIDTH]; o += RET_WIDTH
    rg = w_in[:, o:o + RET_WIDTH]; o += RET_WIDTH
    dt_pad = jnp.zeros((d, LANES_V7X - SSD_HEADS), w_in.dtype)
    main = jnp.concatenate([z, xbc, dq, dk, dv, rq, rk, rv, rg], axis=1)
    w_dt = jnp.concatenate([dt[:, :SSD_HEADS], dt_pad, dt[:, SSD_HEADS:], dt_pad], axis=1)
    return main.astype(bf16), w_dt.astype(bf16)


def _rope_tables(s, n_ctx):
    rows = s // GRID_W
    row = jnp.repeat(jnp.arange(rows, dtype=f32), GRID_W)
    col = jnp.tile(jnp.arange(GRID_W, dtype=f32), rows)
    n_ax = DIFF_HEAD_DIM // 4
    inv_ax = 1.0 / (ROPE_BASE ** (jnp.arange(n_ax, dtype=f32) / n_ax))
    ar = row[:, None] * inv_ax
    ac = col[:, None] * inv_ax
    ang = jnp.concatenate([ar, ar, ac, ac], axis=-1)
    ang = jnp.tile(ang, (1, 2))
    inv_ret = 1.0 / (ROPE_BASE ** jnp.linspace(0.0, 1.0, RET_QK_DIM // 2, dtype=f32))
    ang_c = jnp.arange(n_ctx, dtype=f32)[:, None] * inv_ret
    ang_l = (n_ctx + jnp.arange(s, dtype=f32))[:, None] * inv_ret
    tile_ret = lambda a: jnp.tile(jnp.concatenate([a, a], axis=-1), (1, RET_HEADS))
    ang = jnp.tile(ang, (1, DIFF_HEADS))
    return {
        "diff_l": (jnp.cos(ang), jnp.sin(ang)),
        "diff_c": (jnp.ones((n_ctx, DIFF_WIDTH), f32), jnp.zeros((n_ctx, DIFF_WIDTH), f32)),
        "ret_l": (jnp.cos(tile_ret(ang_l)), jnp.sin(tile_ret(ang_l))),
        "ret_c": (jnp.cos(tile_ret(ang_c)), jnp.sin(tile_ret(ang_c))),
    }


def kernel(x, c, ctx, c_ctx, w_ada, b_ada, w_in, ssd_conv_w, ssd_conv_b, ssd_a_log, ssd_dt_bias, ssd_d, ssd_norm_w,
           diff_lambda, diff_norm_w, ret_decay, ret_norm_w, w_out, ln1_w, ln1_b, ffn_w_up, ffn_conv_w, ffn_conv_b,
           ffn_w_down, ln2_w, ln2_b):
    bsz, s, d = x.shape
    n_ctx = ctx.shape[1]
    depth = w_ada.shape[0]
    alpha = (2.0 * depth) ** 0.25
    assert d == D_MODEL and s % 512 == 0 and n_ctx % ATT_TP == 0 and s % GRID_W == 0

    tabs = _rope_tables(s, n_ctx)
    nrow = -(-(bsz + 1) // SUBLANES_V7X) * SUBLANES_V7X
    cvecs = jnp.zeros((nrow, d), f32).at[:bsz].set(c).at[bsz].set(c_ctx)
    mod_all = _ada_call(cvecs, w_ada, b_ada)

    gw = SSD_HPG * SSD_HEAD_DIM
    zero_ssd = jnp.zeros((bsz, 2, SSD_GROUPS, SSD_STATE, gw), f32)
    zero_ret = jnp.zeros((bsz, 2, RET_HEADS // 2, 2 * RET_QK_DIM, RET_V_DIM), f32)

    xc = ctx
    for li in range(depth):
        last = li == depth - 1
        lam_init = 0.8 - 0.6 * math.exp(-0.3 * li)
        mod = mod_all[li]
        m_lat = [mod[:bsz, k * d:(k + 1) * d].reshape(bsz, 1, d) for k in range(6)]
        m_ctx = [jnp.broadcast_to(mod[bsz, k * d:(k + 1) * d].reshape(1, 1, d), (bsz, 1, d)) for k in range(6)]
        w_main, w_dt = _ext_weight(w_in[li])
        w_out_b = w_out[li].astype(bf16)
        w_up_b = ffn_w_up[li].astype(bf16)
        w_down_b = ffn_w_down[li].astype(bf16)
        d_skip = jnp.repeat(ssd_d[li], SSD_HEAD_DIM).reshape(1, SSD_INNER)

        proj, dt_l = _in_proj_call(x, m_lat[1], m_lat[0], w_main, w_dt)
        proj_c, dt_c = _in_proj_call(xc, m_ctx[1], m_ctx[0], w_main, w_dt)

        xbc_c = _ssd_conv_call(proj_c, ssd_conv_w[li], ssd_conv_b[li])
        ys_c, hs = _ssd_scan_call(xbc_c, dt_c, ssd_a_log[li], ssd_dt_bias[li], zero_ssd)
        xbc = _ssd_conv_call(proj, ssd_conv_w[li], ssd_conv_b[li])
        ys, _ = _ssd_scan_call(xbc, dt_l, ssd_a_log[li], ssd_dt_bias[li], hs)

        yr_c, hr = _ret_scan_call(proj_c, *tabs["ret_c"], ret_decay[li], zero_ret)
        yr, _ = _ret_scan_call(proj, *tabs["ret_l"], ret_decay[li], hr)

        qt_c, k_c, vt_c = _attn_prep_call(proj_c, *tabs["diff_c"])
        qt_l, k_l, vt_l = _attn_prep_call(proj, *tabs["diff_l"])
        k_all = jnp.concatenate([k_l, k_c], axis=1)
        vt_all = jnp.concatenate([vt_l, vt_c], axis=3)
        y_diff = _attn_call(qt_l, k_all, vt_all, diff_lambda[li], diff_norm_w[li], lam_init)

        x = _mix_out_call(x, ys, xbc, proj, y_diff, yr, d_skip, ssd_norm_w[li], ret_norm_w[li], w_out_b,
                          m_lat[2], ln1_w[li], ln1_b[li], alpha)
        x = _ffn_call(x, m_lat[4], m_lat[3], m_lat[5], w_up_b, ffn_conv_w[li], ffn_conv_b[li], w_down_b,
                      ln2_w[li], ln2_b[li], alpha)
        if not last:
            yc_diff = _attn_call(qt_c, k_c, vt_c, diff_lambda[li], diff_norm_w[li], lam_init)
            xc = _mix_out_call(xc, ys_c, xbc_c, proj_c, yc_diff, yr_c, d_skip, ssd_norm_w[li], ret_norm_w[li], w_out_b,
                               m_ctx[2], ln1_w[li], ln1_b[li], alpha)
            xc = _ffn_call(xc, m_ctx[4], m_ctx[3], m_ctx[5], w_up_b, ffn_conv_w[li], ffn_conv_b[li], w_down_b,
                           ln2_w[li], ln2_b[li], alpha)
    return x
```

```python
import functools
import math

import jax
import jax.numpy as jnp
from jax import lax
from jax.experimental import pallas as pl
from jax.experimental.pallas import tpu as pltpu

f32 = jnp.float32
bf16 = jnp.bfloat16
HIGHEST = lax.Precision.HIGHEST

D_MODEL = 1024
GRID_W = 64
CHUNK = 128
ROPE_BASE = 10000.0
SSD_INNER = D_MODEL
SSD_HEAD_DIM = 64
SSD_HEADS = SSD_INNER // SSD_HEAD_DIM
SSD_GROUPS = 2
SSD_HPG = SSD_HEADS // SSD_GROUPS
SSD_STATE = 128
SSD_CONV = 5
SSD_XBC = SSD_INNER + 2 * SSD_GROUPS * SSD_STATE
SSD_COLS = SSD_INNER + SSD_XBC + 2 * SSD_HEADS
DIFF_WIDTH = D_MODEL // 2
DIFF_V_DIM = 128
DIFF_HEADS = DIFF_WIDTH // DIFF_V_DIM
DIFF_HEAD_DIM = DIFF_V_DIM // 2
DIFF_QK = DIFF_HEADS * 2 * DIFF_HEAD_DIM
DIFF_COLS = 2 * DIFF_QK + DIFF_WIDTH
RET_WIDTH = D_MODEL // 2
RET_V_DIM = 128
RET_HEADS = RET_WIDTH // RET_V_DIM
RET_QK_DIM = RET_V_DIM // 2
RET_QK = RET_HEADS * RET_QK_DIM
RET_COLS = 2 * RET_QK + 2 * RET_WIDTH
MIX_WIDTH = 2 * D_MODEL
D_FF = 11 * D_MODEL // 4
FFN_CONV = 3
EPS = 1e-5

LANES_V7X = 128
SUBLANES_V7X = 8
VMEM_LIMIT_V7X = 56 * 1024 * 1024

COL_Z = 0
COL_XBC = 8
COL_DQ = 20
COL_DK = 24
COL_DV = 28
COL_RQ = 32
COL_RK = 34
COL_RV = 36
COL_RG = 40
N_MAIN = 44 * LANES_V7X
NEG_BIG = -1e30


def _cparams(sem):
    return pltpu.CompilerParams(dimension_semantics=sem, vmem_limit_bytes=VMEM_LIMIT_V7X)


def _silu(x):
    return x * (1.0 / (1.0 + jnp.exp(-x)))


def _layernorm(x, w, b):
    mu = jnp.mean(x, axis=-1, keepdims=True)
    xc = x - mu
    var = jnp.mean(xc * xc, axis=-1, keepdims=True)
    return xc * lax.rsqrt(var + EPS) * w + b


def _rotate_half(x, half):
    n = x.shape[-1]
    lane = lax.broadcasted_iota(jnp.int32, x.shape, x.ndim - 1)
    first = (lane & (2 * half - 1)) < half
    return jnp.where(first, -pltpu.roll(x, n - half, x.ndim - 1), pltpu.roll(x, half, x.ndim - 1))


def _run_pipeline(step, n):
    step(0, 0, 1 if n > 1 else None, False)
    n_loop = max((n - 1) // 2 - 1, 0)

    def pair(k, carry):
        step(2 * k + 1, 1, 2 * k + 2, True)
        step(2 * k + 2, 0, 2 * k + 3, True)
        return carry

    lax.fori_loop(0, n_loop, pair, 0)
    for c in range(2 * n_loop + 1, n):
        step(c, c % 2, c + 1 if c + 1 < n else None, True)


def _split2(x):
    hi = x.astype(bf16)
    lo = (x - hi.astype(f32)).astype(bf16)
    return hi, lo


def _ada_kernel(c_ref, w_ref, b_ref, o_ref):
    c = c_ref[...]
    o_ref[...] = jnp.dot(_silu(c), w_ref[...], precision=HIGHEST, preferred_element_type=f32) + b_ref[...]


def _ada_call(cvecs, w_ada, b_ada):
    depth, d, n = w_ada.shape
    r = cvecs.shape[0]
    tn = 1536
    return pl.pallas_call(
        _ada_kernel,
        grid=(depth, n // tn),
        in_specs=[
            pl.BlockSpec((r, d), lambda l, j: (0, 0)),
            pl.BlockSpec((None, d, tn), lambda l, j: (l, 0, j)),
            pl.BlockSpec((None, 1, tn), lambda l, j: (l, 0, j)),
        ],
        out_specs=pl.BlockSpec((None, r, tn), lambda l, j: (l, 0, j)),
        out_shape=jax.ShapeDtypeStruct((depth, r, n), f32),
        compiler_params=_cparams(("parallel", "parallel")),
        name="ada",
    )(cvecs, w_ada, b_ada.reshape(depth, 1, n))


def _in_proj_kernel(x_ref, sc_ref, sh_ref, w_ref, wdt_ref, o_ref, odt_ref):
    xm = (x_ref[...] * (1.0 + sc_ref[...]) + sh_ref[...]).astype(bf16)
    half = w_ref.shape[1] // 2
    for j in range(2):
        cols = slice(j * half, (j + 1) * half)
        o_ref[:, cols] = jnp.dot(xm, w_ref[:, cols], preferred_element_type=f32).astype(o_ref.dtype)
    odt_ref[...] = jnp.dot(xm, wdt_ref[...], preferred_element_type=f32)


def _in_proj_call(x, sc, sh, w_main, w_dt):
    bsz, t, d = x.shape
    n = w_main.shape[1]
    ndt = w_dt.shape[1]
    tm = min(512, t)
    mod = pl.BlockSpec((None, 1, d), lambda b, i: (b, 0, 0))
    return pl.pallas_call(
        _in_proj_kernel,
        grid=(bsz, t // tm),
        in_specs=[
            pl.BlockSpec((None, tm, d), lambda b, i: (b, i, 0)),
            mod, mod,
            pl.BlockSpec((d, n), lambda b, i: (0, 0)),
            pl.BlockSpec((d, ndt), lambda b, i: (0, 0)),
        ],
        out_specs=[
            pl.BlockSpec((None, tm, n), lambda b, i: (b, i, 0)),
            pl.BlockSpec((None, tm, ndt), lambda b, i: (b, i, 0)),
        ],
        out_shape=[
            jax.ShapeDtypeStruct((bsz, t, n), bf16),
            jax.ShapeDtypeStruct((bsz, t, ndt), f32),
        ],
        compiler_params=_cparams(("parallel", "parallel")),
        name="in_proj",
    )(x, sc, sh, w_main, w_dt)


def _ssd_conv_kernel(cur_ref, prev_ref, next_ref, w_ref, b_ref, o_ref, ext_ref, *, tm, nt):
    i = pl.program_id(1)
    h = SUBLANES_V7X
    ext_ref[0:h, :] = jnp.where(i > 0, prev_ref[...].astype(f32)[h:2 * h], 0.0)
    ext_ref[h:h + tm, :] = cur_ref[...].astype(f32)
    ext_ref[h + tm:2 * h + tm, :] = jnp.where(i < nt - 1, next_ref[...].astype(f32)[0:h], 0.0)
    p = SSD_CONV // 2
    acc = ext_ref[h - p:h - p + tm, :] * w_ref[0:1, :]
    for j in range(1, SSD_CONV):
        acc = acc + ext_ref[h - p + j:h - p + j + tm, :] * w_ref[j:j + 1, :]
    o_ref[...] = _silu(acc + b_ref[...]).astype(o_ref.dtype)


def _ssd_conv_call(proj, conv_w, conv_b):
    bsz, t, _ = proj.shape
    tm = min(512, t)
    nt = t // tm
    cw = 512
    nc = SSD_XBC // cw
    c0 = COL_XBC * LANES_V7X // cw
    halo = 2 * SUBLANES_V7X
    hb = tm // halo
    last_hb = t // halo - 1
    return pl.pallas_call(
        functools.partial(_ssd_conv_kernel, tm=tm, nt=nt),
        grid=(bsz, nt, nc),
        in_specs=[
            pl.BlockSpec((None, tm, cw), lambda b, i, c: (b, i, c0 + c)),
            pl.BlockSpec((None, halo, cw), lambda b, i, c: (b, jnp.maximum(i * hb - 1, 0), c0 + c)),
            pl.BlockSpec((None, halo, cw), lambda b, i, c: (b, jnp.minimum((i + 1) * hb, last_hb), c0 + c)),
            pl.BlockSpec((SSD_CONV, cw), lambda b, i, c: (0, c)),
            pl.BlockSpec((1, cw), lambda b, i, c: (0, c)),
        ],
        out_specs=pl.BlockSpec((None, tm, cw), lambda b, i, c: (b, i, c)),
        out_shape=jax.ShapeDtypeStruct((bsz, t, SSD_XBC), bf16),
        scratch_shapes=[pltpu.VMEM((tm + 2 * SUBLANES_V7X, cw), f32)],
        compiler_params=_cparams(("parallel", "parallel", "parallel")),
        name="ssd_conv",
    )(proj, proj, proj, conv_w, conv_b.reshape(1, SSD_XBC))


def _ssd_scan_kernel(cq_ref, bk_ref, x_ref, dt_ref, alog_ref, dtb_ref, tri_ref, e_ref, h0_ref,
                     y_ref, hout_ref, h_ref, *, ns, cps):
    d = pl.program_id(1)
    j = pl.program_id(2)
    L = CHUNK
    gw = SSD_HPG * SSD_HEAD_DIM

    @pl.when(j == 0)
    def _():
        h_ref[...] = h0_ref[...]

    tri = tri_ref[...]
    mask = tri > 0.5
    tri_b = tri.astype(bf16)
    a_neg = -jnp.exp(alog_ref[...])
    quad_w = 4 * SSD_HEAD_DIM
    lane = lax.broadcasted_iota(jnp.int32, (L, quad_w), 1)
    in_quarter = [(lane >= r * SSD_HEAD_DIM) & (lane < (r + 1) * SSD_HEAD_DIM) for r in range(4)]

    def one_chunk(rows):
        sp = jax.nn.softplus(dt_ref[rows, :] + dtb_ref[...])
        la = sp * a_neg
        la_hi = la.astype(bf16)
        rem = la - la_hi.astype(f32)
        la_mid = rem.astype(bf16)
        la_lo = (rem - la_mid.astype(f32)).astype(bf16)
        c3 = jnp.dot(tri_b, jnp.concatenate([la_hi, la_mid, la_lo], axis=1), preferred_element_type=f32)
        cum = c3[:, 0:LANES_V7X] + c3[:, LANES_V7X:2 * LANES_V7X] + c3[:, 2 * LANES_V7X:3 * LANES_V7X]
        cum_t = cum.T
        sp_t = sp.T
        tot = jnp.sum(la, axis=0, keepdims=True)
        w_end = jnp.exp(tot - cum) * sp
        e_cum = jnp.exp(cum)
        dec = jnp.broadcast_to(jnp.exp(tot), (SUBLANES_V7X, LANES_V7X))
        pieces = _split2(w_end) + _split2(e_cum) + _split2(dec)
        r = jnp.dot(jnp.concatenate(pieces, axis=0), e_ref[...], preferred_element_type=f32)
        w_x = r[0:L] + r[L:2 * L]
        ec_x = r[2 * L:3 * L] + r[3 * L:4 * L]
        dec_x = r[4 * L:4 * L + 1] + r[4 * L + SUBLANES_V7X:4 * L + SUBLANES_V7X + 1]

        x = x_ref[rows, :].astype(f32)
        xw = (x * w_x).astype(bf16)
        for g in range(SSD_GROUPS):
            qg = cq_ref[rows, g * SSD_STATE:(g + 1) * SSD_STATE]
            kg = bk_ref[rows, g * SSD_STATE:(g + 1) * SSD_STATE]
            scores = lax.dot_general(qg, kg, (((1,), (1,)), ((), ())), preferred_element_type=f32)
            kg_t = kg.astype(f32).T.astype(bf16)
            h_old = h_ref[g]
            y_off = jnp.dot(qg, h_old.astype(bf16), preferred_element_type=f32) * ec_x[:, g * gw:(g + 1) * gw]
            h_ref[g] = dec_x[:, g * gw:(g + 1) * gw] * h_old + jnp.dot(
                kg_t, xw[:, g * gw:(g + 1) * gw], preferred_element_type=f32)
            outs = []
            for qd in range(SSD_HPG // 4):
                ms = []
                for hh in range(4):
                    head = g * SSD_HPG + 4 * qd + hh
                    seg = cum[:, head:head + 1] - cum_t[head:head + 1, :]
                    decay = jnp.exp(jnp.where(mask, seg, NEG_BIG))
                    ms.append((scores * decay * sp_t[head:head + 1, :]).astype(bf16))
                c_lo = g * gw + qd * quad_w
                xq = x[:, c_lo:c_lo + quad_w]
                rhs = jnp.concatenate([jnp.where(in_quarter[r], xq, 0.0) for r in range(4)], axis=0).astype(bf16)
                outs.append(jnp.dot(jnp.concatenate(ms, axis=1), rhs, preferred_element_type=f32))
            y_ref[rows, g * gw:(g + 1) * gw] = (jnp.concatenate(outs, axis=1) + y_off).astype(y_ref.dtype)

    for t in range(cps):
        off = (t + d * (cps - 1 - 2 * t)) * L
        one_chunk(pl.ds(pl.multiple_of(off, L), L))

    @pl.when(j == ns - 1)
    def _():
        hout_ref[...] = h_ref[...]


def _chunks_per_step(nc):
    return 4 if nc % 4 == 0 else (2 if nc % 2 == 0 else 1)


def _ssd_scan_call(xbc, dt_raw, a_log, dt_bias, h0):
    bsz, t, _ = xbc.shape
    cps = _chunks_per_step(t // CHUNK)
    L = CHUNK
    rows = cps * L
    ns = t // rows
    gw = SSD_HPG * SSD_HEAD_DIM

    def chunk(d, j):
        return j + d * (ns - 1 - 2 * j)

    alog = jnp.zeros((2, 1, LANES_V7X), f32).at[:, 0, :SSD_HEADS].set(a_log)
    dtb = jnp.zeros((2, 1, LANES_V7X), f32).at[:, 0, :SSD_HEADS].set(dt_bias)
    idx = jnp.arange(L)
    tri = jnp.stack([(idx[None, :] <= idx[:, None]), (idx[None, :] >= idx[:, None])]).astype(f32)
    head_col = jnp.arange(LANES_V7X)[:, None]
    heads = (jnp.arange(SSD_INNER) // SSD_HEAD_DIM)[None, :]
    expand = (head_col == heads).astype(bf16)
    bcol = SSD_INNER // (SSD_GROUPS * SSD_STATE)
    return pl.pallas_call(
        functools.partial(_ssd_scan_kernel, ns=ns, cps=cps),
        grid=(bsz, 2, ns),
        in_specs=[
            pl.BlockSpec((None, rows, SSD_GROUPS * SSD_STATE), lambda b, d, j: (b, chunk(d, j), bcol + 1)),
            pl.BlockSpec((None, rows, SSD_GROUPS * SSD_STATE), lambda b, d, j: (b, chunk(d, j), bcol)),
            pl.BlockSpec((None, rows, SSD_INNER), lambda b, d, j: (b, chunk(d, j), 0)),
            pl.BlockSpec((None, rows, LANES_V7X), lambda b, d, j: (b, chunk(d, j), d)),
            pl.BlockSpec((None, 1, LANES_V7X), lambda b, d, j: (d, 0, 0)),
            pl.BlockSpec((None, 1, LANES_V7X), lambda b, d, j: (d, 0, 0)),
            pl.BlockSpec((None, L, L), lambda b, d, j: (d, 0, 0)),
            pl.BlockSpec((LANES_V7X, SSD_INNER), lambda b, d, j: (0, 0)),
            pl.BlockSpec((None, None, SSD_GROUPS, SSD_STATE, gw), lambda b, d, j: (b, d, 0, 0, 0)),
        ],
        out_specs=[
            pl.BlockSpec((None, None, rows, SSD_INNER), lambda b, d, j: (b, d, chunk(d, j), 0)),
            pl.BlockSpec((None, None, SSD_GROUPS, SSD_STATE, gw), lambda b, d, j: (b, d, 0, 0, 0)),
        ],
        out_shape=[
            jax.ShapeDtypeStruct((bsz, 2, t, SSD_INNER), bf16),
            jax.ShapeDtypeStruct((bsz, 2, SSD_GROUPS, SSD_STATE, gw), f32),
        ],
        scratch_shapes=[pltpu.VMEM((SSD_GROUPS, SSD_STATE, gw), f32)],
        compiler_params=_cparams(("parallel", "parallel", "arbitrary")),
        name="ssd_scan",
    )(xbc, xbc, xbc, dt_raw, alog, dtb, tri, expand, h0)


def _ret_scan_kernel(q_ref, k_ref, v_ref, cos_ref, sin_ref, rd_ref, h0_ref,
                     y_ref, hout_ref, h_ref, dec_ref, ecx_ref, tex_ref, dst_ref, *, ns, cps):
    d = pl.program_id(1)
    j = pl.program_id(2)
    L = CHUNK
    npair = RET_HEADS // 2

    @pl.when(j == 0)
    def _():
        h_ref[...] = h0_ref[...]
        lg = -jnp.exp(rd_ref[...])
        li = lax.broadcasted_iota(jnp.int32, (L, L), 0)
        si = lax.broadcasted_iota(jnp.int32, (L, L), 1)
        dist = jnp.where(d == 0, li - si, si - li)
        causal = dist >= 0
        distf = dist.astype(f32)
        lane = lax.broadcasted_iota(jnp.int32, (1, RET_WIDTH), 1)
        lgx = jnp.zeros((1, RET_WIDTH), f32)
        for h in range(RET_HEADS):
            lgh = lg[:, h:h + 1]
            dec_ref[h] = jnp.exp(jnp.where(causal, distf * lgh, NEG_BIG))
            lgx = jnp.where((lane >= h * RET_V_DIM) & (lane < (h + 1) * RET_V_DIM), lgh, lgx)
        pos = lax.broadcasted_iota(jnp.int32, (L, 1), 0)
        steps = jnp.where(d == 0, pos + 1, L - pos).astype(f32)
        cumx = steps * lgx
        totx = float(L) * lgx
        ecx_ref[...] = jnp.exp(cumx)
        tex_ref[...] = jnp.exp(totx - cumx)
        rowi = lax.broadcasted_iota(jnp.int32, (2 * RET_QK_DIM, RET_V_DIM), 0)
        for pp in range(npair):
            d0 = jnp.exp(float(L) * lg[:, 2 * pp:2 * pp + 1])
            d1 = jnp.exp(float(L) * lg[:, 2 * pp + 1:2 * pp + 2])
            dst_ref[pp] = jnp.where(rowi < RET_QK_DIM, d0, d1)

    ecx = ecx_ref[...]
    tex = tex_ref[...]
    hi_half = lax.broadcasted_iota(jnp.int32, (L, LANES_V7X), 1) >= RET_QK_DIM
    zeros_v = jnp.zeros((L, RET_V_DIM), bf16)

    def one_chunk(rows):
        cos = cos_ref[rows, :]
        sin = sin_ref[rows, :]
        q = q_ref[rows, :].astype(f32)
        k = k_ref[rows, :].astype(f32)
        q = q * cos + _rotate_half(q, RET_QK_DIM // 2) * sin
        k = (k * cos + _rotate_half(k, RET_QK_DIM // 2) * sin) * (RET_QK_DIM ** -0.5)
        vb = v_ref[rows, :]
        vw = (vb.astype(f32) * tex).astype(bf16)
        for pp in range(npair):
            qp = q[:, pp * LANES_V7X:(pp + 1) * LANES_V7X]
            kp = k[:, pp * LANES_V7X:(pp + 1) * LANES_V7X]
            kp_b = kp.astype(bf16)
            kp_t = kp.T.astype(bf16)
            r_old = h_ref[pp]
            r_b = r_old.astype(bf16)
            ms, offs = [], []
            for hh in range(2):
                h = 2 * pp + hh
                qm = (jnp.where(hi_half, qp, 0.0) if hh else jnp.where(hi_half, 0.0, qp)).astype(bf16)
                scores = lax.dot_general(qm, kp_b, (((1,), (1,)), ((), ())), preferred_element_type=f32)
                ms.append((scores * dec_ref[h]).astype(bf16))
                offs.append(jnp.dot(qm, r_b, preferred_element_type=f32))
            va = vb[:, (2 * pp) * RET_V_DIM:(2 * pp + 1) * RET_V_DIM]
            vc = vb[:, (2 * pp + 1) * RET_V_DIM:(2 * pp + 2) * RET_V_DIM]
            rhs = jnp.concatenate([jnp.concatenate([va, zeros_v], axis=1),
                                   jnp.concatenate([zeros_v, vc], axis=1)], axis=0)
            y_diag = jnp.dot(jnp.concatenate(ms, axis=1), rhs, preferred_element_type=f32)
            c0 = 2 * pp * RET_V_DIM
            y_ref[rows, c0:c0 + 2 * RET_V_DIM] = (
                y_diag + jnp.concatenate(offs, axis=1) * ecx[:, c0:c0 + 2 * RET_V_DIM]).astype(y_ref.dtype)
            upd = jnp.dot(kp_t, vw[:, c0:c0 + 2 * RET_V_DIM], preferred_element_type=f32)
            new = jnp.concatenate([upd[0:RET_QK_DIM, 0:RET_V_DIM],
                                   upd[RET_QK_DIM:2 * RET_QK_DIM, RET_V_DIM:2 * RET_V_DIM]], axis=0)
            h_ref[pp] = dst_ref[pp] * r_old + new

    for t in range(cps):
        off = (t + d * (cps - 1 - 2 * t)) * L
        one_chunk(pl.ds(pl.multiple_of(off, L), L))

    @pl.when(j == ns - 1)
    def _():
        hout_ref[...] = h_ref[...]


def _ret_scan_call(proj, cos, sin, decay_raw, h0):
    bsz, t, _ = proj.shape
    cps = _chunks_per_step(t // CHUNK)
    L = CHUNK
    rows = cps * L
    ns = t // rows
    npair = RET_HEADS // 2

    def chunk(d, j):
        return j + d * (ns - 1 - 2 * j)

    rd = jnp.zeros((2, 1, LANES_V7X), f32).at[:, 0, :RET_HEADS].set(decay_raw)
    qk = lambda c: pl.BlockSpec((None, rows, RET_QK), lambda b, d, j: (b, chunk(d, j), c // 2))
    tab = pl.BlockSpec((rows, RET_QK), lambda b, d, j: (chunk(d, j), 0))
    st = pl.BlockSpec((None, None, npair, 2 * RET_QK_DIM, RET_V_DIM), lambda b, d, j: (b, d, 0, 0, 0))
    return pl.pallas_call(
        functools.partial(_ret_scan_kernel, ns=ns, cps=cps),
        grid=(bsz, 2, ns),
        in_specs=[
            qk(COL_RQ), qk(COL_RK),
            pl.BlockSpec((None, rows, RET_WIDTH), lambda b, d, j: (b, chunk(d, j), COL_RV // 4)),
            tab, tab,
            pl.BlockSpec((None, 1, LANES_V7X), lambda b, d, j: (d, 0, 0)),
            st,
        ],
        out_specs=[
            pl.BlockSpec((None, None, rows, RET_WIDTH), lambda b, d, j: (b, d, chunk(d, j), 0)),
            st,
        ],
        out_shape=[
            jax.ShapeDtypeStruct((bsz, 2, t, RET_WIDTH), bf16),
            jax.ShapeDtypeStruct((bsz, 2, npair, 2 * RET_QK_DIM, RET_V_DIM), f32),
        ],
        scratch_shapes=[
            pltpu.VMEM((npair, 2 * RET_QK_DIM, RET_V_DIM), f32),
            pltpu.VMEM((RET_HEADS, L, L), f32),
            pltpu.VMEM((L, RET_WIDTH), f32),
            pltpu.VMEM((L, RET_WIDTH), f32),
            pltpu.VMEM((npair, 2 * RET_QK_DIM, RET_V_DIM), f32),
        ],
        compiler_params=_cparams(("parallel", "parallel", "arbitrary")),
        name="ret_scan",
    )(proj, proj, proj, cos, sin, rd, h0)


ATT_TP = 256
ATT_TQS = 256
ATT_VROWS = DIFF_V_DIM + 16
LOG2E = 1.4426950408889634


def _attn_prep_kernel(q_ref, k_ref, v_ref, cos_ref, sin_ref, qt_ref, ko_ref, vt_ref):
    cos = cos_ref[...]
    sin = sin_ref[...]
    ax = DIFF_HEAD_DIM // 4
    q = q_ref[...].astype(f32)
    k = k_ref[...].astype(f32)
    q = (q * cos + _rotate_half(q, ax) * sin) * (DIFF_HEAD_DIM ** -0.5 * LOG2E)
    ko_ref[...] = (k * cos + _rotate_half(k, ax) * sin).astype(bf16)
    v = v_ref[...].astype(f32)
    ones = jnp.ones((ATT_VROWS - DIFF_V_DIM, vt_ref.shape[2]), bf16)
    for h in range(DIFF_HEADS):
        cols = slice(h * LANES_V7X, (h + 1) * LANES_V7X)
        qt_ref[h] = q[:, cols].T.astype(bf16)
        vt_ref[h, 0:DIFF_V_DIM, :] = v[:, cols].T.astype(bf16)
        vt_ref[h, DIFF_V_DIM:ATT_VROWS, :] = ones


def _attn_prep_call(proj, cos, sin):
    bsz, t, _ = proj.shape
    tp = ATT_TP
    nt = t // tp
    w = DIFF_WIDTH
    col = lambda c0: pl.BlockSpec((None, tp, w), lambda b, i: (b, i, c0 * LANES_V7X // w))
    tab = pl.BlockSpec((tp, w), lambda b, i: (i, 0))
    return pl.pallas_call(
        _attn_prep_kernel,
        grid=(bsz, nt),
        in_specs=[col(COL_DQ), col(COL_DK), col(COL_DV), tab, tab],
        out_specs=[
            pl.BlockSpec((None, DIFF_HEADS, LANES_V7X, tp), lambda b, i: (b, 0, 0, i)),
            pl.BlockSpec((None, tp, w), lambda b, i: (b, i, 0)),
            pl.BlockSpec((None, DIFF_HEADS, ATT_VROWS, tp), lambda b, i: (b, 0, 0, i)),
        ],
        out_shape=[
            jax.ShapeDtypeStruct((bsz, DIFF_HEADS, 2 * DIFF_HEAD_DIM, t), bf16),
            jax.ShapeDtypeStruct((bsz, t, DIFF_WIDTH), bf16),
            jax.ShapeDtypeStruct((bsz, DIFF_HEADS, ATT_VROWS, t), bf16),
        ],
        compiler_params=_cparams(("parallel", "parallel")),
        name="attn_prep",
    )(proj, proj, proj, cos, sin)


def _attn_kernel(qt_ref, k_ref, vt_ref, lam_ref, nw_ref, o_ref, q2_ref, m_ref, acc_ref, s_ref, p_ref, a_ref, mx_ref,
                 *, nk, tk, lam_init):
    nsub = qt_ref.shape[1] // ATT_TQS
    row = lax.broadcasted_iota(jnp.int32, (2 * DIFF_HEAD_DIM, ATT_TQS), 0)
    for sb in range(nsub):
        qt = qt_ref[:, sb * ATT_TQS:(sb + 1) * ATT_TQS]
        q2_ref[0, sb] = jnp.where(row < DIFF_HEAD_DIM, qt, jnp.zeros_like(qt))
        q2_ref[1, sb] = jnp.where(row < DIFF_HEAD_DIM, jnp.zeros_like(qt), qt)
    m_ref[...] = jnp.full(m_ref.shape, NEG_BIG, f32)
    acc_ref[...] = jnp.zeros(acc_ref.shape, f32)
    chains = [(mi, sb) for mi in range(2) for sb in range(nsub)]

    def key_chunk(c):
        return k_ref[pl.ds(pl.multiple_of(c * tk, ATT_TP), tk), :]

    def value_chunk(c):
        return vt_ref[:, pl.ds(pl.multiple_of(c * tk, ATT_TP), tk)]

    def step(c, par, c_next, has_prev, softmax=True):
        oth = 1 - par
        kn = None if c_next is None else key_chunk(c_next)
        vt = value_chunk(c - 1) if has_prev else None
        for mi, sb in chains:
            if kn is not None:
                scores(kn, oth, mi, sb)
            if has_prev:
                acc_ref[mi, sb] = (a_ref[oth, mi, sb] * acc_ref[mi, sb]
                                   + jnp.dot(vt, p_ref[oth, mi, sb], preferred_element_type=f32))
            if softmax:
                m_old = m_ref[mi, sb]
                m_new = jnp.maximum(m_old, mx_ref[par, mi, sb])
                p_ref[par, mi, sb] = jnp.exp2(s_ref[par, mi, sb] - m_new).astype(bf16)
                a_ref[par, mi, sb] = jnp.exp2(m_old - m_new)
                m_ref[mi, sb] = m_new

    def scores(kc, slot, mi, sb):
        s = jnp.dot(kc, q2_ref[mi, sb], preferred_element_type=f32)
        s_ref[slot, mi, sb] = s
        mx_ref[slot, mi, sb] = jnp.max(s, axis=0, keepdims=True)

    k0 = key_chunk(0)
    for mi, sb in chains:
        scores(k0, 0, mi, sb)
    _run_pipeline(step, nk)
    step(nk, nk % 2, None, True, softmax=False)

    lp = lam_ref[...]
    lam = (jnp.exp(jnp.sum(lp[0:1] * lp[1:2], axis=1, keepdims=True))
           - jnp.exp(jnp.sum(lp[2:3] * lp[3:4], axis=1, keepdims=True)) + lam_init)
    nv = DIFF_V_DIM
    for sb in range(nsub):
        ot = (acc_ref[0, sb, 0:nv, :] / acc_ref[0, sb, nv:nv + 1, :]
              - lam * (acc_ref[1, sb, 0:nv, :] / acc_ref[1, sb, nv:nv + 1, :]))
        o = ot.T
        ms = jnp.mean(o * o, axis=-1, keepdims=True)
        o_ref[sb * ATT_TQS:(sb + 1) * ATT_TQS, :] = (
            o * lax.rsqrt(ms + EPS) * nw_ref[...] * (1.0 - lam_init)).astype(o_ref.dtype)


def _attn_call(qt, k_all, vt_all, lam_p, norm_w, lam_init):
    bsz, nh, _, sq = qt.shape
    sk = k_all.shape[1]
    tk = 3 * ATT_TP if sk % (3 * ATT_TP) == 0 else ATT_TP
    nk = sk // tk
    tq = min(1024, sq)
    nsub = tq // ATT_TQS
    return pl.pallas_call(
        functools.partial(_attn_kernel, nk=nk, tk=tk, lam_init=lam_init),
        grid=(bsz, nh, sq // tq),
        in_specs=[
            pl.BlockSpec((None, None, 2 * DIFF_HEAD_DIM, tq), lambda b, h, i: (b, h, 0, i)),
            pl.BlockSpec((None, sk, LANES_V7X), lambda b, h, i: (b, 0, h)),
            pl.BlockSpec((None, None, ATT_VROWS, sk), lambda b, h, i: (b, h, 0, 0)),
            pl.BlockSpec((4, DIFF_HEAD_DIM), lambda b, h, i: (0, 0)),
            pl.BlockSpec((1, DIFF_V_DIM), lambda b, h, i: (0, 0)),
        ],
        out_specs=pl.BlockSpec((None, tq, DIFF_V_DIM), lambda b, h, i: (b, i, h)),
        out_shape=jax.ShapeDtypeStruct((bsz, sq, DIFF_WIDTH), bf16),
        scratch_shapes=[
            pltpu.VMEM((2, nsub, 2 * DIFF_HEAD_DIM, ATT_TQS), bf16),
            pltpu.VMEM((2, nsub, 1, ATT_TQS), f32),
            pltpu.VMEM((2, nsub, ATT_VROWS, ATT_TQS), f32),
            pltpu.VMEM((2, 2, nsub, tk, ATT_TQS), f32),
            pltpu.VMEM((2, 2, nsub, tk, ATT_TQS), bf16),
            pltpu.VMEM((2, 2, nsub, 1, ATT_TQS), f32),
            pltpu.VMEM((2, 2, nsub, 1, ATT_TQS), f32),
        ],
        compiler_params=_cparams(("parallel", "parallel", "parallel")),
        name="diff_attn",
    )(qt, k_all, vt_all, lam_p, norm_w.reshape(1, DIFF_V_DIM))


def _mix_out_kernel(x_ref, ysf_ref, ysb_ref, xs_ref, z_ref, yd_ref, yrf_ref, yrb_ref, g_ref,
                    dsk_ref, snw_ref, rnw_ref, wo_ref, ga_ref, lw_ref, lb_ref, o_ref, *, alpha):
    y = ysf_ref[...].astype(f32) + ysb_ref[...].astype(f32) + xs_ref[...].astype(f32) * dsk_ref[...]
    y = y * _silu(z_ref[...].astype(f32))
    y_ssd = y * lax.rsqrt(jnp.mean(y * y, axis=-1, keepdims=True) + EPS) * snw_ref[...]
    yr = yrf_ref[...].astype(f32) + yrb_ref[...].astype(f32)
    gate = _silu(g_ref[...].astype(f32))
    rets = []
    for h in range(RET_HEADS):
        yh = yr[:, h * RET_V_DIM:(h + 1) * RET_V_DIM]
        mu = jnp.mean(yh, axis=-1, keepdims=True)
        yc = yh - mu
        var = jnp.mean(yc * yc, axis=-1, keepdims=True)
        rets.append(yc * lax.rsqrt(var + EPS) * rnw_ref[...])
    y_ret = jnp.concatenate(rets, axis=1) * gate
    ycat = jnp.concatenate([y_ssd.astype(bf16), yd_ref[...], y_ret.astype(bf16)], axis=1)
    mixed = jnp.dot(ycat, wo_ref[...], preferred_element_type=f32)
    o_ref[...] = _layernorm(alpha * x_ref[...] + ga_ref[...] * mixed, lw_ref[...], lb_ref[...])


def _mix_out_call(x, y_ssd, xbc, proj, y_diff, y_ret, d_skip, ssd_norm_w, ret_norm_w, w_out, g_a, ln_w, ln_b, alpha):
    bsz, t, d = x.shape
    tm = min(512, t)
    row = lambda width, c: pl.BlockSpec((None, tm, width), lambda b, i: (b, i, c))
    two = lambda width, dd: pl.BlockSpec((None, None, tm, width), lambda b, i: (b, dd, i, 0))
    vec = lambda width: pl.BlockSpec((1, width), lambda b, i: (0, 0))
    return pl.pallas_call(
        functools.partial(_mix_out_kernel, alpha=alpha),
        grid=(bsz, t // tm),
        in_specs=[
            row(d, 0),
            two(SSD_INNER, 0), two(SSD_INNER, 1),
            row(SSD_INNER, 0),
            row(SSD_INNER, COL_Z * LANES_V7X // SSD_INNER),
            row(DIFF_WIDTH, 0),
            two(RET_WIDTH, 0), two(RET_WIDTH, 1),
            row(RET_WIDTH, COL_RG * LANES_V7X // RET_WIDTH),
            vec(SSD_INNER), vec(SSD_INNER), vec(RET_V_DIM),
            pl.BlockSpec((MIX_WIDTH, d), lambda b, i: (0, 0)),
            pl.BlockSpec((None, 1, d), lambda b, i: (b, 0, 0)),
            vec(d), vec(d),
        ],
        out_specs=row(d, 0),
        out_shape=jax.ShapeDtypeStruct((bsz, t, d), f32),
        compiler_params=_cparams(("parallel", "parallel")),
        name="mix_out",
    )(x, y_ssd, y_ssd, xbc, proj, y_diff, y_ret, y_ret, proj,
      d_skip, ssd_norm_w.reshape(1, -1), ret_norm_w.reshape(1, -1), w_out, g_a, ln_w.reshape(1, -1), ln_b.reshape(1, -1))


def _gelu_exact(x):
    return 0.5 * x * (1.0 + lax.erf(x * (2.0 ** -0.5)))


FFN_SPLIT = 1


def _ffn_kernel(x_ref, xp_ref, xn_ref, sc_ref, sh_ref, gf_ref, wu_ref, wv_ref, cw_ref, cb_ref, wd_ref,
                lw_ref, lb_ref, o_ref, xe_ref, ue_ref, acc_ref, *, tm, nt, nf, alpha):
    i = pl.program_id(1)
    c = pl.program_id(2)
    h = SUBLANES_V7X
    hm = tm // FFN_SPLIT

    @pl.when(c == 0)
    def _():
        scale = 1.0 + sc_ref[...]
        shift = sh_ref[...]
        xe_ref[0:h, :] = jnp.where(i > 0, xp_ref[...] * scale + shift, 0.0)
        xe_ref[h:h + tm, :] = x_ref[...] * scale + shift
        xe_ref[h + tm:2 * h + tm, :] = jnp.where(i < nt - 1, xn_ref[...] * scale + shift, 0.0)
        acc_ref[...] = jnp.zeros(acc_ref.shape, f32)

    vs = []
    for r in range(FFN_SPLIT):
        r0 = r * hm
        ue_ref[r] = jnp.dot(xe_ref[r0:r0 + hm + 2 * h, :].astype(bf16), wu_ref[...], preferred_element_type=f32)
        vs.append(jnp.dot(xe_ref[h + r0:h + r0 + hm, :].astype(bf16), wv_ref[...], preferred_element_type=f32))
    p = FFN_CONV // 2
    for r in range(FFN_SPLIT):
        u = cb_ref[...] + ue_ref[r, h - p:h - p + hm, :] * cw_ref[0:1, :]
        for j in range(1, FFN_CONV):
            u = u + ue_ref[r, h - p + j:h - p + j + hm, :] * cw_ref[j:j + 1, :]
        gated = (_gelu_exact(u) * vs[r]).astype(bf16)
        acc_ref[r * hm:(r + 1) * hm, :] += jnp.dot(gated, wd_ref[...], preferred_element_type=f32)

    @pl.when(c == nf - 1)
    def _():
        o_ref[...] = _layernorm(alpha * x_ref[...] + gf_ref[...] * acc_ref[...], lw_ref[...], lb_ref[...])


def _ffn_call(x, sc, sh, gf, w_up, conv_w, conv_b, w_down, ln_w, ln_b, alpha):
    bsz, t, d = x.shape
    tm = min(512, t)
    nt = t // tm
    tf = D_FF // 2
    nf = D_FF // tf
    hb = tm // SUBLANES_V7X
    last_hb = t // SUBLANES_V7X - 1
    mod = pl.BlockSpec((None, 1, d), lambda b, i, c: (b, 0, 0))
    vec = pl.BlockSpec((1, d), lambda b, i, c: (0, 0))
    return pl.pallas_call(
        functools.partial(_ffn_kernel, tm=tm, nt=nt, nf=nf, alpha=alpha),
        grid=(bsz, nt, nf),
        in_specs=[
            pl.BlockSpec((None, tm, d), lambda b, i, c: (b, i, 0)),
            pl.BlockSpec((None, SUBLANES_V7X, d), lambda b, i, c: (b, jnp.maximum(i * hb - 1, 0), 0)),
            pl.BlockSpec((None, SUBLANES_V7X, d), lambda b, i, c: (b, jnp.minimum((i + 1) * hb, last_hb), 0)),
            mod, mod, mod,
            pl.BlockSpec((d, tf), lambda b, i, c: (0, c)),
            pl.BlockSpec((d, tf), lambda b, i, c: (0, nf + c)),
            pl.BlockSpec((FFN_CONV, tf), lambda b, i, c: (0, c)),
            pl.BlockSpec((1, tf), lambda b, i, c: (0, c)),
            pl.BlockSpec((tf, d), lambda b, i, c: (c, 0)),
            vec, vec,
        ],
        out_specs=pl.BlockSpec((None, tm, d), lambda b, i, c: (b, i, 0)),
        out_shape=jax.ShapeDtypeStruct((bsz, t, d), f32),
        scratch_shapes=[
            pltpu.VMEM((tm + 2 * SUBLANES_V7X, d), f32),
            pltpu.VMEM((FFN_SPLIT, tm // FFN_SPLIT + 2 * SUBLANES_V7X, tf), f32),
            pltpu.VMEM((tm, d), f32),
        ],
        compiler_params=_cparams(("parallel", "parallel", "arbitrary")),
        name="conv_ffn",
    )(x, x, x, sc, sh, gf, w_up, w_up, conv_w, conv_b.reshape(1, -1), w_down, ln_w.reshape(1, -1), ln_b.reshape(1, -1))


def _ext_weight(w_in):
    d = w_in.shape[0]
    o = 0
    z = w_in[:, o:o + SSD_INNER]; o += SSD_INNER
    xbc = w_in[:, o:o + SSD_XBC]; o += SSD_XBC
    dt = w_in[:, o:o + 2 * SSD_HEADS]; o += 2 * SSD_HEADS
    dq = w_in[:, o:o + DIFF_QK]; o += DIFF_QK
    dk = w_in[:, o:o + DIFF_QK]; o += DIFF_QK
    dv = w_in[:, o:o + DIFF_WIDTH]; o += DIFF_WIDTH
    rq = w_in[:, o:o + RET_QK]; o += RET_QK
    rk = w_in[:, o:o + RET_QK]; o += RET_QK
    rv = w_in[:, o:o + RET_WIDTH]; o += RET_WIDTH
    rg = w_in[:, o:o + RET_WIDTH]; o += RET_WIDTH
    dt_pad = jnp.zeros((d, LANES_V7X - SSD_HEADS), w_in.dtype)
    main = jnp.concatenate([z, xbc, dq, dk, dv, rq, rk, rv, rg], axis=1)
    w_dt = jnp.concatenate([dt[:, :SSD_HEADS], dt_pad, dt[:, SSD_HEADS:], dt_pad], axis=1)
    return main.astype(bf16), w_dt.astype(bf16)


def _rope_tables(s, n_ctx):
    rows = s // GRID_W
    row = jnp.repeat(jnp.arange(rows, dtype=f32), GRID_W)
    col = jnp.tile(jnp.arange(GRID_W, dtype=f32), rows)
    n_ax = DIFF_HEAD_DIM // 4
    inv_ax = 1.0 / (ROPE_BASE ** (jnp.arange(n_ax, dtype=f32) / n_ax))
    ar = row[:, None] * inv_ax
    ac = col[:, None] * inv_ax
    ang = jnp.concatenate([ar, ar, ac, ac], axis=-1)
    ang = jnp.tile(ang, (1, 2))
    inv_ret = 1.0 / (ROPE_BASE ** jnp.linspace(0.0, 1.0, RET_QK_DIM // 2, dtype=f32))
    ang_c = jnp.arange(n_ctx, dtype=f32)[:, None] * inv_ret
    ang_l = (n_ctx + jnp.arange(s, dtype=f32))[:, None] * inv_ret
    tile_ret = lambda a: jnp.tile(jnp.concatenate([a, a], axis=-1), (1, RET_HEADS))
    ang = jnp.tile(ang, (1, DIFF_HEADS))
    return {
        "diff_l": (jnp.cos(ang), jnp.sin(ang)),
        "diff_c": (jnp.ones((n_ctx, DIFF_WIDTH), f32), jnp.zeros((n_ctx, DIFF_WIDTH), f32)),
        "ret_l": (jnp.cos(tile_ret(ang_l)), jnp.sin(tile_ret(ang_l))),
        "ret_c": (jnp.cos(tile_ret(ang_c)), jnp.sin(tile_ret(ang_c))),
    }


def kernel(x, c, ctx, c_ctx, w_ada, b_ada, w_in, ssd_conv_w, ssd_conv_b, ssd_a_log, ssd_dt_bias, ssd_d, ssd_norm_w,
           diff_lambda, diff_norm_w, ret_decay, ret_norm_w, w_out, ln1_w, ln1_b, ffn_w_up, ffn_conv_w, ffn_conv_b,
           ffn_w_down, ln2_w, ln2_b):
    bsz, s, d = x.shape
    n_ctx = ctx.shape[1]
    depth = w_ada.shape[0]
    alpha = (2.0 * depth) ** 0.25
    assert d == D_MODEL and s % 512 == 0 and n_ctx % ATT_TP == 0 and s % GRID_W == 0

    tabs = _rope_tables(s, n_ctx)
    nrow = -(-(bsz + 1) // SUBLANES_V7X) * SUBLANES_V7X
    cvecs = jnp.zeros((nrow, d), f32).at[:bsz].set(c).at[bsz].set(c_ctx)
    mod_all = _ada_call(cvecs, w_ada, b_ada)

    gw = SSD_HPG * SSD_HEAD_DIM
    zero_ssd = jnp.zeros((bsz, 2, SSD_GROUPS, SSD_STATE, gw), f32)
    zero_ret = jnp.zeros((bsz, 2, RET_HEADS // 2, 2 * RET_QK_DIM, RET_V_DIM), f32)

    xc = ctx
    for li in range(depth):
        last = li == depth - 1
        lam_init = 0.8 - 0.6 * math.exp(-0.3 * li)
        mod = mod_all[li]
        m_lat = [mod[:bsz, k * d:(k + 1) * d].reshape(bsz, 1, d) for k in range(6)]
        m_ctx = [jnp.broadcast_to(mod[bsz, k * d:(k + 1) * d].reshape(1, 1, d), (bsz, 1, d)) for k in range(6)]
        w_main, w_dt = _ext_weight(w_in[li])
        w_out_b = w_out[li].astype(bf16)
        w_up_b = ffn_w_up[li].astype(bf16)
        w_down_b = ffn_w_down[li].astype(bf16)
        d_skip = jnp.repeat(ssd_d[li], SSD_HEAD_DIM).reshape(1, SSD_INNER)

        proj, dt_l = _in_proj_call(x, m_lat[1], m_lat[0], w_main, w_dt)
        proj_c, dt_c = _in_proj_call(xc, m_ctx[1], m_ctx[0], w_main, w_dt)

        xbc_c = _ssd_conv_call(proj_c, ssd_conv_w[li], ssd_conv_b[li])
        ys_c, hs = _ssd_scan_call(xbc_c, dt_c, ssd_a_log[li], ssd_dt_bias[li], zero_ssd)
        xbc = _ssd_conv_call(proj, ssd_conv_w[li], ssd_conv_b[li])
        ys, _ = _ssd_scan_call(xbc, dt_l, ssd_a_log[li], ssd_dt_bias[li], hs)

        yr_c, hr = _ret_scan_call(proj_c, *tabs["ret_c"], ret_decay[li], zero_ret)
        yr, _ = _ret_scan_call(proj, *tabs["ret_l"], ret_decay[li], hr)

        qt_c, k_c, vt_c = _attn_prep_call(proj_c, *tabs["diff_c"])
        qt_l, k_l, vt_l = _attn_prep_call(proj, *tabs["diff_l"])
        k_all = jnp.concatenate([k_l, k_c], axis=1)
        vt_all = jnp.concatenate([vt_l, vt_c], axis=3)
        y_diff = _attn_call(qt_l, k_all, vt_all, diff_lambda[li], diff_norm_w[li], lam_init)

        x = _mix_out_call(x, ys, xbc, proj, y_diff, yr, d_skip, ssd_norm_w[li], ret_norm_w[li], w_out_b,
                          m_lat[2], ln1_w[li], ln1_b[li], alpha)
        x = _ffn_call(x, m_lat[4], m_lat[3], m_lat[5], w_up_b, ffn_conv_w[li], ffn_conv_b[li], w_down_b,
                      ln2_w[li], ln2_b[li], alpha)
        if not last:
            yc_diff = _attn_call(qt_c, k_c, vt_c, diff_lambda[li], diff_norm_w[li], lam_init)
            xc = _mix_out_call(xc, ys_c, xbc_c, proj_c, yc_diff, yr_c, d_skip, ssd_norm_w[li], ret_norm_w[li], w_out_b,
                               m_ctx[2], ln1_w[li], ln1_b[li], alpha)
            xc = _ffn_call(xc, m_ctx[4], m_ctx[3], m_ctx[5], w_up_b, ffn_conv_w[li], ffn_conv_b[li], w_down_b,
                           ln2_w[li], ln2_b[li], alpha)
    return x
```

```python
import functools
import math

import jax
import jax.numpy as jnp
from jax import lax
from jax.experimental import pallas as pl
from jax.experimental.pallas import tpu as pltpu

f32 = jnp.float32
bf16 = jnp.bfloat16
HIGHEST = lax.Precision.HIGHEST

D_MODEL = 1024
GRID_W = 64
CHUNK = 128
ROPE_BASE = 10000.0
SSD_INNER = D_MODEL
SSD_HEAD_DIM = 64
SSD_HEADS = SSD_INNER // SSD_HEAD_DIM
SSD_GROUPS = 2
SSD_HPG = SSD_HEADS // SSD_GROUPS
SSD_STATE = 128
SSD_CONV = 5
SSD_XBC = SSD_INNER + 2 * SSD_GROUPS * SSD_STATE
SSD_COLS = SSD_INNER + SSD_XBC + 2 * SSD_HEADS
DIFF_WIDTH = D_MODEL // 2
DIFF_V_DIM = 128
DIFF_HEADS = DIFF_WIDTH // DIFF_V_DIM
DIFF_HEAD_DIM = DIFF_V_DIM // 2
DIFF_QK = DIFF_HEADS * 2 * DIFF_HEAD_DIM
DIFF_COLS = 2 * DIFF_QK + DIFF_WIDTH
RET_WIDTH = D_MODEL // 2
RET_V_DIM = 128
RET_HEADS = RET_WIDTH // RET_V_DIM
RET_QK_DIM = RET_V_DIM // 2
RET_QK = RET_HEADS * RET_QK_DIM
RET_COLS = 2 * RET_QK + 2 * RET_WIDTH
MIX_WIDTH = 2 * D_MODEL
D_FF = 11 * D_MODEL // 4
FFN_CONV = 3
EPS = 1e-5

LANES_V7X = 128
SUBLANES_V7X = 8
VMEM_LIMIT_V7X = 56 * 1024 * 1024

COL_Z = 0
COL_DQ = 8
COL_DK = 12
COL_DV = 16
COL_RQ = 20
COL_RK = 22
COL_RV = 24
COL_RG = 28
N_MAIN = 32 * LANES_V7X
NEG_BIG = -1e30


def _cparams(sem):
    return pltpu.CompilerParams(dimension_semantics=sem, vmem_limit_bytes=VMEM_LIMIT_V7X)


def _silu(x):
    return x * (1.0 / (1.0 + jnp.exp(-x)))


def _layernorm(x, w, b):
    mu = jnp.mean(x, axis=-1, keepdims=True)
    xc = x - mu
    var = jnp.mean(xc * xc, axis=-1, keepdims=True)
    return xc * lax.rsqrt(var + EPS) * w + b


def _rotate_half(x, half):
    n = x.shape[-1]
    lane = lax.broadcasted_iota(jnp.int32, x.shape, x.ndim - 1)
    first = (lane & (2 * half - 1)) < half
    return jnp.where(first, -pltpu.roll(x, n - half, x.ndim - 1), pltpu.roll(x, half, x.ndim - 1))


PIPE_BODY = 2


def _run_pipeline(step, n):
    step(0, 0, 1 if n > 1 else None, False)
    n_loop = max((n - 2) // PIPE_BODY, 0)

    def body(k, carry):
        for u in range(PIPE_BODY):
            c = PIPE_BODY * k + 1 + u
            step(c, (1 + u) % 2, c + 1, True)
        return carry

    lax.fori_loop(0, n_loop, body, 0)
    for c in range(PIPE_BODY * n_loop + 1, n):
        step(c, c % 2, c + 1 if c + 1 < n else None, True)


def _split2(x):
    hi = x.astype(bf16)
    lo = (x - hi.astype(f32)).astype(bf16)
    return hi, lo


def _ada_kernel(c_ref, w_ref, b_ref, o_ref):
    c = c_ref[...]
    o_ref[...] = jnp.dot(_silu(c), w_ref[...], precision=HIGHEST, preferred_element_type=f32) + b_ref[...]


def _ada_call(cvecs, w_ada, b_ada):
    depth, d, n = w_ada.shape
    r = cvecs.shape[0]
    tn = 1536
    return pl.pallas_call(
        _ada_kernel,
        grid=(depth, n // tn),
        in_specs=[
            pl.BlockSpec((r, d), lambda l, j: (0, 0)),
            pl.BlockSpec((None, d, tn), lambda l, j: (l, 0, j)),
            pl.BlockSpec((None, 1, tn), lambda l, j: (l, 0, j)),
        ],
        out_specs=pl.BlockSpec((None, r, tn), lambda l, j: (l, 0, j)),
        out_shape=jax.ShapeDtypeStruct((depth, r, n), f32),
        compiler_params=_cparams(("parallel", "parallel")),
        name="ada",
    )(cvecs, w_ada, b_ada.reshape(depth, 1, n))


IN_PROJ_CW = 256
IN_PROJ_PIECE = 1024


def _in_proj_kernel(x_ref, xp_ref, xn_ref, sc_ref, sh_ref, w_ref, wx_ref, wdt_ref, cw_ref, cb_ref,
                    o_ref, oxbc_ref, odt_ref, xe_ref, *, tm, nt):
    i = pl.program_id(1)
    h = SUBLANES_V7X
    scale = 1.0 + sc_ref[...]
    shift = sh_ref[...]
    xe_ref[0:h, :] = xp_ref[...] * scale + shift
    xe_ref[h:h + tm, :] = x_ref[...] * scale + shift
    xe_ref[h + tm:2 * h + tm, :] = xn_ref[...] * scale + shift

    n = tm + 2 * h
    rowi = lax.broadcasted_iota(jnp.int32, (n, 1), 0)
    inside = ((rowi >= h) | (i > 0)) & ((rowi < h + tm) | (i < nt - 1))
    xe_all = xe_ref[...].astype(bf16)
    p = SSD_CONV // 2
    for c in range(SSD_XBC // IN_PROJ_CW):
        cols = slice(c * IN_PROJ_CW, (c + 1) * IN_PROJ_CW)
        ext = jnp.where(inside, jnp.dot(xe_all, wx_ref[:, cols], preferred_element_type=f32), 0.0)
        acc = cb_ref[:, cols] + ext[h:h + tm] * cw_ref[p:p + 1, cols]
        for j in range(SSD_CONV):
            if j != p:
                acc = acc + pltpu.roll(ext, (p - j) % n, 0)[h:h + tm] * cw_ref[j:j + 1, cols]
        oxbc_ref[:, cols] = _silu(acc).astype(oxbc_ref.dtype)

    xm = xe_ref[h:h + tm, :].astype(bf16)
    for c in range(w_ref.shape[1] // IN_PROJ_PIECE):
        cols = slice(c * IN_PROJ_PIECE, (c + 1) * IN_PROJ_PIECE)
        o_ref[:, cols] = jnp.dot(xm, w_ref[:, cols], preferred_element_type=f32).astype(o_ref.dtype)
    odt_ref[...] = jnp.dot(xm, wdt_ref[...], preferred_element_type=f32)


def _in_proj_call(x, sc, sh, w_main, w_xbc, w_dt, conv_w, conv_b):
    bsz, t, d = x.shape
    n = w_main.shape[1]
    ndt = w_dt.shape[1]
    tm = min(512, t)
    nt = t // tm
    hb = tm // SUBLANES_V7X
    last_hb = t // SUBLANES_V7X - 1
    mod = pl.BlockSpec((None, 1, d), lambda b, i: (b, 0, 0))
    whole = lambda a: pl.BlockSpec(a.shape, lambda b, i: (0,) * a.ndim)
    conv_b = conv_b.reshape(1, SSD_XBC)
    row = lambda width: pl.BlockSpec((None, tm, width), lambda b, i: (b, i, 0))
    return pl.pallas_call(
        functools.partial(_in_proj_kernel, tm=tm, nt=nt),
        grid=(bsz, nt),
        in_specs=[
            row(d),
            pl.BlockSpec((None, SUBLANES_V7X, d), lambda b, i: (b, jnp.maximum(i * hb - 1, 0), 0)),
            pl.BlockSpec((None, SUBLANES_V7X, d), lambda b, i: (b, jnp.minimum((i + 1) * hb, last_hb), 0)),
            mod, mod,
            whole(w_main), whole(w_xbc), whole(w_dt), whole(conv_w), whole(conv_b),
        ],
        out_specs=[row(n), row(SSD_XBC), row(ndt)],
        out_shape=[
            jax.ShapeDtypeStruct((bsz, t, n), bf16),
            jax.ShapeDtypeStruct((bsz, t, SSD_XBC), bf16),
            jax.ShapeDtypeStruct((bsz, t, ndt), f32),
        ],
        scratch_shapes=[pltpu.VMEM((tm + 2 * SUBLANES_V7X, d), f32)],
        compiler_params=_cparams(("parallel", "parallel")),
        name="in_proj",
    )(x, x, x, sc, sh, w_main, w_xbc, w_dt, conv_w, conv_b)


def _ssd_scan_kernel(cq_ref, bk_ref, x_ref, dt_ref, alog_ref, dtb_ref, tri_ref, e_ref, h0_ref,
                     y_ref, hout_ref, h_ref, *, ns, cps):
    d = pl.program_id(1)
    j = pl.program_id(2)
    L = CHUNK
    gw = SSD_HPG * SSD_HEAD_DIM

    @pl.when(j == 0)
    def _():
        h_ref[...] = h0_ref[...]

    tri = tri_ref[...]
    mask = tri > 0.5
    tri_b = tri.astype(bf16)
    a_neg = -jnp.exp(alog_ref[...])
    quad_w = 4 * SSD_HEAD_DIM
    lane = lax.broadcasted_iota(jnp.int32, (L, quad_w), 1)
    in_quarter = [(lane >= r * SSD_HEAD_DIM) & (lane < (r + 1) * SSD_HEAD_DIM) for r in range(4)]

    def one_chunk(rows):
        sp = jax.nn.softplus(dt_ref[rows, :] + dtb_ref[...])
        la = sp * a_neg
        la_hi = la.astype(bf16)
        rem = la - la_hi.astype(f32)
        la_mid = rem.astype(bf16)
        la_lo = (rem - la_mid.astype(f32)).astype(bf16)
        c3 = jnp.dot(tri_b, jnp.concatenate([la_hi, la_mid, la_lo], axis=1), preferred_element_type=f32)
        cum = c3[:, 0:LANES_V7X] + c3[:, LANES_V7X:2 * LANES_V7X] + c3[:, 2 * LANES_V7X:3 * LANES_V7X]
        cum_t = cum.T
        sp_t = sp.T
        tot = jnp.sum(la, axis=0, keepdims=True)
        w_end = jnp.exp(tot - cum) * sp
        e_cum = jnp.exp(cum)
        dec = jnp.broadcast_to(jnp.exp(tot), (SUBLANES_V7X, LANES_V7X))
        pieces = _split2(w_end) + _split2(e_cum) + _split2(dec)
        r = jnp.dot(jnp.concatenate(pieces, axis=0), e_ref[...], preferred_element_type=f32)
        w_x = r[0:L] + r[L:2 * L]
        ec_x = r[2 * L:3 * L] + r[3 * L:4 * L]
        dec_x = r[4 * L:4 * L + 1] + r[4 * L + SUBLANES_V7X:4 * L + SUBLANES_V7X + 1]

        x = x_ref[rows, :].astype(f32)
        xw = (x * w_x).astype(bf16)
        for g in range(SSD_GROUPS):
            qg = cq_ref[rows, g * SSD_STATE:(g + 1) * SSD_STATE]
            kg = bk_ref[rows, g * SSD_STATE:(g + 1) * SSD_STATE]
            scores = lax.dot_general(qg, kg, (((1,), (1,)), ((), ())), preferred_element_type=f32)
            kg_t = kg.astype(f32).T.astype(bf16)
            h_old = h_ref[g]
            y_off = jnp.dot(qg, h_old.astype(bf16), preferred_element_type=f32) * ec_x[:, g * gw:(g + 1) * gw]
            h_ref[g] = dec_x[:, g * gw:(g + 1) * gw] * h_old + jnp.dot(
                kg_t, xw[:, g * gw:(g + 1) * gw], preferred_element_type=f32)
            outs = []
            for qd in range(SSD_HPG // 4):
                ms = []
                for hh in range(4):
                    head = g * SSD_HPG + 4 * qd + hh
                    seg = cum[:, head:head + 1] - cum_t[head:head + 1, :]
                    decay = jnp.exp(jnp.where(mask, seg, NEG_BIG))
                    ms.append((scores * decay * sp_t[head:head + 1, :]).astype(bf16))
                c_lo = g * gw + qd * quad_w
                xq = x[:, c_lo:c_lo + quad_w]
                rhs = jnp.concatenate([jnp.where(in_quarter[r], xq, 0.0) for r in range(4)], axis=0).astype(bf16)
                outs.append(jnp.dot(jnp.concatenate(ms, axis=1), rhs, preferred_element_type=f32))
            y_ref[rows, g * gw:(g + 1) * gw] = (jnp.concatenate(outs, axis=1) + y_off).astype(y_ref.dtype)

    for t in range(cps):
        off = (t + d * (cps - 1 - 2 * t)) * L
        one_chunk(pl.ds(pl.multiple_of(off, L), L))

    @pl.when(j == ns - 1)
    def _():
        hout_ref[...] = h_ref[...]


def _chunks_per_step(nc):
    return 4 if nc % 4 == 0 else (2 if nc % 2 == 0 else 1)


def _ssd_scan_call(xbc, dt_raw, a_log, dt_bias, h0):
    bsz, t, _ = xbc.shape
    cps = _chunks_per_step(t // CHUNK)
    L = CHUNK
    rows = cps * L
    ns = t // rows
    gw = SSD_HPG * SSD_HEAD_DIM

    def chunk(d, j):
        return j + d * (ns - 1 - 2 * j)

    alog = jnp.zeros((2, 1, LANES_V7X), f32).at[:, 0, :SSD_HEADS].set(a_log)
    dtb = jnp.zeros((2, 1, LANES_V7X), f32).at[:, 0, :SSD_HEADS].set(dt_bias)
    idx = jnp.arange(L)
    tri = jnp.stack([(idx[None, :] <= idx[:, None]), (idx[None, :] >= idx[:, None])]).astype(f32)
    head_col = jnp.arange(LANES_V7X)[:, None]
    heads = (jnp.arange(SSD_INNER) // SSD_HEAD_DIM)[None, :]
    expand = (head_col == heads).astype(bf16)
    bcol = SSD_INNER // (SSD_GROUPS * SSD_STATE)
    return pl.pallas_call(
        functools.partial(_ssd_scan_kernel, ns=ns, cps=cps),
        grid=(bsz, 2, ns),
        in_specs=[
            pl.BlockSpec((None, rows, SSD_GROUPS * SSD_STATE), lambda b, d, j: (b, chunk(d, j), bcol + 1)),
            pl.BlockSpec((None, rows, SSD_GROUPS * SSD_STATE), lambda b, d, j: (b, chunk(d, j), bcol)),
            pl.BlockSpec((None, rows, SSD_INNER), lambda b, d, j: (b, chunk(d, j), 0)),
            pl.BlockSpec((None, rows, LANES_V7X), lambda b, d, j: (b, chunk(d, j), d)),
            pl.BlockSpec((None, 1, LANES_V7X), lambda b, d, j: (d, 0, 0)),
            pl.BlockSpec((None, 1, LANES_V7X), lambda b, d, j: (d, 0, 0)),
            pl.BlockSpec((None, L, L), lambda b, d, j: (d, 0, 0)),
            pl.BlockSpec((LANES_V7X, SSD_INNER), lambda b, d, j: (0, 0)),
            pl.BlockSpec((None, None, SSD_GROUPS, SSD_STATE, gw), lambda b, d, j: (b, d, 0, 0, 0)),
        ],
        out_specs=[
            pl.BlockSpec((None, None, rows, SSD_INNER), lambda b, d, j: (b, d, chunk(d, j), 0)),
            pl.BlockSpec((None, None, SSD_GROUPS, SSD_STATE, gw), lambda b, d, j: (b, d, 0, 0, 0)),
        ],
        out_shape=[
            jax.ShapeDtypeStruct((bsz, 2, t, SSD_INNER), bf16),
            jax.ShapeDtypeStruct((bsz, 2, SSD_GROUPS, SSD_STATE, gw), f32),
        ],
        scratch_shapes=[pltpu.VMEM((SSD_GROUPS, SSD_STATE, gw), f32)],
        compiler_params=_cparams(("parallel", "parallel", "arbitrary")),
        name="ssd_scan",
    )(xbc, xbc, xbc, dt_raw, alog, dtb, tri, expand, h0)


def _ret_scan_kernel(q_ref, k_ref, v_ref, cos_ref, sin_ref, rd_ref, h0_ref,
                     y_ref, hout_ref, h_ref, dec_ref, ecx_ref, tex_ref, dst_ref, *, ns, cps):
    d = pl.program_id(1)
    j = pl.program_id(2)
    L = CHUNK
    npair = RET_HEADS // 2

    @pl.when(j == 0)
    def _():
        h_ref[...] = h0_ref[...]
        lg = -jnp.exp(rd_ref[...])
        li = lax.broadcasted_iota(jnp.int32, (L, L), 0)
        si = lax.broadcasted_iota(jnp.int32, (L, L), 1)
        dist = jnp.where(d == 0, li - si, si - li)
        causal = dist >= 0
        distf = dist.astype(f32)
        lane = lax.broadcasted_iota(jnp.int32, (1, RET_WIDTH), 1)
        lgx = jnp.zeros((1, RET_WIDTH), f32)
        for h in range(RET_HEADS):
            lgh = lg[:, h:h + 1]
            dec_ref[h] = jnp.exp(jnp.where(causal, distf * lgh, NEG_BIG))
            lgx = jnp.where((lane >= h * RET_V_DIM) & (lane < (h + 1) * RET_V_DIM), lgh, lgx)
        pos = lax.broadcasted_iota(jnp.int32, (L, 1), 0)
        steps = jnp.where(d == 0, pos + 1, L - pos).astype(f32)
        cumx = steps * lgx
        totx = float(L) * lgx
        ecx_ref[...] = jnp.exp(cumx)
        tex_ref[...] = jnp.exp(totx - cumx)
        rowi = lax.broadcasted_iota(jnp.int32, (2 * RET_QK_DIM, RET_V_DIM), 0)
        for pp in range(npair):
            d0 = jnp.exp(float(L) * lg[:, 2 * pp:2 * pp + 1])
            d1 = jnp.exp(float(L) * lg[:, 2 * pp + 1:2 * pp + 2])
            dst_ref[pp] = jnp.where(rowi < RET_QK_DIM, d0, d1)

    ecx = ecx_ref[...]
    tex = tex_ref[...]
    hi_half = lax.broadcasted_iota(jnp.int32, (L, LANES_V7X), 1) >= RET_QK_DIM
    zeros_v = jnp.zeros((L, RET_V_DIM), bf16)

    def one_chunk(rows):
        cos = cos_ref[rows, :]
        sin = sin_ref[rows, :]
        q = q_ref[rows, :].astype(f32)
        k = k_ref[rows, :].astype(f32)
        q = q * cos + _rotate_half(q, RET_QK_DIM // 2) * sin
        k = (k * cos + _rotate_half(k, RET_QK_DIM // 2) * sin) * (RET_QK_DIM ** -0.5)
        vb = v_ref[rows, :]
        vw = (vb.astype(f32) * tex).astype(bf16)
        for pp in range(npair):
            qp = q[:, pp * LANES_V7X:(pp + 1) * LANES_V7X]
            kp = k[:, pp * LANES_V7X:(pp + 1) * LANES_V7X]
            kp_b = kp.astype(bf16)
            kp_t = kp.T.astype(bf16)
            r_old = h_ref[pp]
            r_b = r_old.astype(bf16)
            ms, offs = [], []
            for hh in range(2):
                h = 2 * pp + hh
                qm = (jnp.where(hi_half, qp, 0.0) if hh else jnp.where(hi_half, 0.0, qp)).astype(bf16)
                scores = lax.dot_general(qm, kp_b, (((1,), (1,)), ((), ())), preferred_element_type=f32)
                ms.append((scores * dec_ref[h]).astype(bf16))
                offs.append(jnp.dot(qm, r_b, preferred_element_type=f32))
            va = vb[:, (2 * pp) * RET_V_DIM:(2 * pp + 1) * RET_V_DIM]
            vc = vb[:, (2 * pp + 1) * RET_V_DIM:(2 * pp + 2) * RET_V_DIM]
            rhs = jnp.concatenate([jnp.concatenate([va, zeros_v], axis=1),
                                   jnp.concatenate([zeros_v, vc], axis=1)], axis=0)
            y_diag = jnp.dot(jnp.concatenate(ms, axis=1), rhs, preferred_element_type=f32)
            c0 = 2 * pp * RET_V_DIM
            y_ref[rows, c0:c0 + 2 * RET_V_DIM] = (
                y_diag + jnp.concatenate(offs, axis=1) * ecx[:, c0:c0 + 2 * RET_V_DIM]).astype(y_ref.dtype)
            upd = jnp.dot(kp_t, vw[:, c0:c0 + 2 * RET_V_DIM], preferred_element_type=f32)
            new = jnp.concatenate([upd[0:RET_QK_DIM, 0:RET_V_DIM],
                                   upd[RET_QK_DIM:2 * RET_QK_DIM, RET_V_DIM:2 * RET_V_DIM]], axis=0)
            h_ref[pp] = dst_ref[pp] * r_old + new

    for t in range(cps):
        off = (t + d * (cps - 1 - 2 * t)) * L
        one_chunk(pl.ds(pl.multiple_of(off, L), L))

    @pl.when(j == ns - 1)
    def _():
        hout_ref[...] = h_ref[...]


def _ret_scan_call(proj, cos, sin, decay_raw, h0):
    bsz, t, _ = proj.shape
    cps = _chunks_per_step(t // CHUNK)
    L = CHUNK
    rows = cps * L
    ns = t // rows
    npair = RET_HEADS // 2

    def chunk(d, j):
        return j + d * (ns - 1 - 2 * j)

    rd = jnp.zeros((2, 1, LANES_V7X), f32).at[:, 0, :RET_HEADS].set(decay_raw)
    qk = lambda c: pl.BlockSpec((None, rows, RET_QK), lambda b, d, j: (b, chunk(d, j), c // 2))
    tab = pl.BlockSpec((rows, RET_QK), lambda b, d, j: (chunk(d, j), 0))
    st = pl.BlockSpec((None, None, npair, 2 * RET_QK_DIM, RET_V_DIM), lambda b, d, j: (b, d, 0, 0, 0))
    return pl.pallas_call(
        functools.partial(_ret_scan_kernel, ns=ns, cps=cps),
        grid=(bsz, 2, ns),
        in_specs=[
            qk(COL_RQ), qk(COL_RK),
            pl.BlockSpec((None, rows, RET_WIDTH), lambda b, d, j: (b, chunk(d, j), COL_RV // 4)),
            tab, tab,
            pl.BlockSpec((None, 1, LANES_V7X), lambda b, d, j: (d, 0, 0)),
            st,
        ],
        out_specs=[
            pl.BlockSpec((None, None, rows, RET_WIDTH), lambda b, d, j: (b, d, chunk(d, j), 0)),
            st,
        ],
        out_shape=[
            jax.ShapeDtypeStruct((bsz, 2, t, RET_WIDTH), bf16),
            jax.ShapeDtypeStruct((bsz, 2, npair, 2 * RET_QK_DIM, RET_V_DIM), f32),
        ],
        scratch_shapes=[
            pltpu.VMEM((npair, 2 * RET_QK_DIM, RET_V_DIM), f32),
            pltpu.VMEM((RET_HEADS, L, L), f32),
            pltpu.VMEM((L, RET_WIDTH), f32),
            pltpu.VMEM((L, RET_WIDTH), f32),
            pltpu.VMEM((npair, 2 * RET_QK_DIM, RET_V_DIM), f32),
        ],
        compiler_params=_cparams(("parallel", "parallel", "arbitrary")),
        name="ret_scan",
    )(proj, proj, proj, cos, sin, rd, h0)


ATT_TP = 256
ATT_TQS = 256
ATT_VROWS = DIFF_V_DIM + 16
LOG2E = 1.4426950408889634


def _attn_prep_kernel(q_ref, k_ref, v_ref, cos_ref, sin_ref, qt_ref, ko_ref, vt_ref):
    cos = cos_ref[...]
    sin = sin_ref[...]
    ax = DIFF_HEAD_DIM // 4
    q = q_ref[...].astype(f32)
    k = k_ref[...].astype(f32)
    q = (q * cos + _rotate_half(q, ax) * sin) * (DIFF_HEAD_DIM ** -0.5 * LOG2E)
    ko_ref[...] = (k * cos + _rotate_half(k, ax) * sin).astype(bf16)
    v = v_ref[...].astype(f32)
    ones = jnp.ones((ATT_VROWS - DIFF_V_DIM, vt_ref.shape[2]), bf16)
    for h in range(DIFF_HEADS):
        cols = slice(h * LANES_V7X, (h + 1) * LANES_V7X)
        qt_ref[h] = q[:, cols].T.astype(bf16)
        vt_ref[h, 0:DIFF_V_DIM, :] = v[:, cols].T.astype(bf16)
        vt_ref[h, DIFF_V_DIM:ATT_VROWS, :] = ones


def _attn_prep_call(proj, cos, sin):
    bsz, t, _ = proj.shape
    tp = ATT_TP
    nt = t // tp
    w = DIFF_WIDTH
    col = lambda c0: pl.BlockSpec((None, tp, w), lambda b, i: (b, i, c0 * LANES_V7X // w))
    tab = pl.BlockSpec((tp, w), lambda b, i: (i, 0))
    return pl.pallas_call(
        _attn_prep_kernel,
        grid=(bsz, nt),
        in_specs=[col(COL_DQ), col(COL_DK), col(COL_DV), tab, tab],
        out_specs=[
            pl.BlockSpec((None, DIFF_HEADS, LANES_V7X, tp), lambda b, i: (b, 0, 0, i)),
            pl.BlockSpec((None, tp, w), lambda b, i: (b, i, 0)),
            pl.BlockSpec((None, DIFF_HEADS, ATT_VROWS, tp), lambda b, i: (b, 0, 0, i)),
        ],
        out_shape=[
            jax.ShapeDtypeStruct((bsz, DIFF_HEADS, 2 * DIFF_HEAD_DIM, t), bf16),
            jax.ShapeDtypeStruct((bsz, t, DIFF_WIDTH), bf16),
            jax.ShapeDtypeStruct((bsz, DIFF_HEADS, ATT_VROWS, t), bf16),
        ],
        compiler_params=_cparams(("parallel", "parallel")),
        name="attn_prep",
    )(proj, proj, proj, cos, sin)


def _attn_kernel(qt_ref, k_ref, vt_ref, lam_ref, nw_ref, o_ref, q2_ref, m_ref, acc_ref, s_ref, p_ref, a_ref, mx_ref,
                 *, nk, tk, lam_init):
    nsub, tqs = q2_ref.shape[1], q2_ref.shape[3]
    row = lax.broadcasted_iota(jnp.int32, (2 * DIFF_HEAD_DIM, tqs), 0)
    for sb in range(nsub):
        qt = qt_ref[:, sb * tqs:(sb + 1) * tqs]
        q2_ref[0, sb] = jnp.where(row < DIFF_HEAD_DIM, qt, jnp.zeros_like(qt))
        q2_ref[1, sb] = jnp.where(row < DIFF_HEAD_DIM, jnp.zeros_like(qt), qt)
    m_ref[...] = jnp.full(m_ref.shape, NEG_BIG, f32)
    acc_ref[...] = jnp.zeros(acc_ref.shape, f32)
    chains = [(mi, sb) for mi in range(2) for sb in range(nsub)]

    def key_chunk(c):
        return k_ref[pl.ds(pl.multiple_of(c * tk, ATT_TP), tk), :]

    def value_chunk(c):
        return vt_ref[:, pl.ds(pl.multiple_of(c * tk, ATT_TP), tk)]

    def step(c, par, c_next, has_prev, softmax=True):
        oth = 1 - par
        kn = None if c_next is None else key_chunk(c_next)
        vt = value_chunk(c - 1) if has_prev else None
        for mi, sb in chains:
            if kn is not None:
                scores(kn, oth, mi, sb)
            if has_prev:
                acc_ref[mi, sb] = (a_ref[oth, mi, sb] * acc_ref[mi, sb]
                                   + jnp.dot(vt, p_ref[oth, mi, sb], preferred_element_type=f32))
            if softmax:
                m_old = m_ref[mi, sb]
                m_new = jnp.maximum(m_old, mx_ref[par, mi, sb])
                p_ref[par, mi, sb] = jnp.exp2(s_ref[par, mi, sb] - m_new).astype(bf16)
                a_ref[par, mi, sb] = jnp.exp2(m_old - m_new)
                m_ref[mi, sb] = m_new

    def scores(kc, slot, mi, sb):
        s = jnp.dot(kc, q2_ref[mi, sb], preferred_element_type=f32)
        s_ref[slot, mi, sb] = s
        mx_ref[slot, mi, sb] = jnp.max(s, axis=0, keepdims=True)

    k0 = key_chunk(0)
    for mi, sb in chains:
        scores(k0, 0, mi, sb)
    _run_pipeline(step, nk)
    step(nk, nk % 2, None, True, softmax=False)

    lp = lam_ref[...]
    lam = (jnp.exp(jnp.sum(lp[0:1] * lp[1:2], axis=1, keepdims=True))
           - jnp.exp(jnp.sum(lp[2:3] * lp[3:4], axis=1, keepdims=True)) + lam_init)
    nv = DIFF_V_DIM
    for sb in range(nsub):
        ot = (acc_ref[0, sb, 0:nv, :] / acc_ref[0, sb, nv:nv + 1, :]
              - lam * (acc_ref[1, sb, 0:nv, :] / acc_ref[1, sb, nv:nv + 1, :]))
        o = ot.T
        ms = jnp.mean(o * o, axis=-1, keepdims=True)
        o_ref[sb * tqs:(sb + 1) * tqs, :] = (
            o * lax.rsqrt(ms + EPS) * nw_ref[...] * (1.0 - lam_init)).astype(o_ref.dtype)


def _attn_call(qt, k_all, vt_all, lam_p, norm_w, lam_init):
    bsz, nh, _, sq = qt.shape
    sk = k_all.shape[1]
    tk = 3 * ATT_TP if sk % (3 * ATT_TP) == 0 else ATT_TP
    nk = sk // tk
    tq = min(1024, sq)
    tqs = min(ATT_TQS, tq)
    nsub = tq // tqs
    return pl.pallas_call(
        functools.partial(_attn_kernel, nk=nk, tk=tk, lam_init=lam_init),
        grid=(bsz, nh, sq // tq),
        in_specs=[
            pl.BlockSpec((None, None, 2 * DIFF_HEAD_DIM, tq), lambda b, h, i: (b, h, 0, i)),
            pl.BlockSpec((None, sk, LANES_V7X), lambda b, h, i: (b, 0, h)),
            pl.BlockSpec((None, None, ATT_VROWS, sk), lambda b, h, i: (b, h, 0, 0)),
            pl.BlockSpec((4, DIFF_HEAD_DIM), lambda b, h, i: (0, 0)),
            pl.BlockSpec((1, DIFF_V_DIM), lambda b, h, i: (0, 0)),
        ],
        out_specs=pl.BlockSpec((None, tq, DIFF_V_DIM), lambda b, h, i: (b, i, h)),
        out_shape=jax.ShapeDtypeStruct((bsz, sq, DIFF_WIDTH), bf16),
        scratch_shapes=[
            pltpu.VMEM((2, nsub, 2 * DIFF_HEAD_DIM, tqs), bf16),
            pltpu.VMEM((2, nsub, 1, tqs), f32),
            pltpu.VMEM((2, nsub, ATT_VROWS, tqs), f32),
            pltpu.VMEM((2, 2, nsub, tk, tqs), f32),
            pltpu.VMEM((2, 2, nsub, tk, tqs), bf16),
            pltpu.VMEM((2, 2, nsub, 1, tqs), f32),
            pltpu.VMEM((2, 2, nsub, 1, tqs), f32),
        ],
        compiler_params=_cparams(("parallel", "parallel", "parallel")),
        name="diff_attn",
    )(qt, k_all, vt_all, lam_p, norm_w.reshape(1, DIFF_V_DIM))


def _mix_out_kernel(x_ref, ysf_ref, ysb_ref, xs_ref, z_ref, yd_ref, yrf_ref, yrb_ref, g_ref,
                    dsk_ref, snw_ref, rnw_ref, wo_ref, ga_ref, lw_ref, lb_ref, o_ref, *, alpha):
    y = ysf_ref[...].astype(f32) + ysb_ref[...].astype(f32) + xs_ref[...].astype(f32) * dsk_ref[...]
    y = y * _silu(z_ref[...].astype(f32))
    y_ssd = y * lax.rsqrt(jnp.mean(y * y, axis=-1, keepdims=True) + EPS) * snw_ref[...]
    yr = yrf_ref[...].astype(f32) + yrb_ref[...].astype(f32)
    gate = _silu(g_ref[...].astype(f32))
    rets = []
    for h in range(RET_HEADS):
        yh = yr[:, h * RET_V_DIM:(h + 1) * RET_V_DIM]
        mu = jnp.mean(yh, axis=-1, keepdims=True)
        yc = yh - mu
        var = jnp.mean(yc * yc, axis=-1, keepdims=True)
        rets.append(yc * lax.rsqrt(var + EPS) * rnw_ref[...])
    y_ret = jnp.concatenate(rets, axis=1) * gate
    ycat = jnp.concatenate([y_ssd.astype(bf16), yd_ref[...], y_ret.astype(bf16)], axis=1)
    mixed = jnp.dot(ycat, wo_ref[...], preferred_element_type=f32)
    o_ref[...] = _layernorm(alpha * x_ref[...] + ga_ref[...] * mixed, lw_ref[...], lb_ref[...])


def _mix_out_call(x, y_ssd, xbc, proj, y_diff, y_ret, d_skip, ssd_norm_w, ret_norm_w, w_out, g_a, ln_w, ln_b, alpha):
    bsz, t, d = x.shape
    tm = min(512, t)
    row = lambda width, c: pl.BlockSpec((None, tm, width), lambda b, i: (b, i, c))
    two = lambda width, dd: pl.BlockSpec((None, None, tm, width), lambda b, i: (b, dd, i, 0))
    vec = lambda width: pl.BlockSpec((1, width), lambda b, i: (0, 0))
    return pl.pallas_call(
        functools.partial(_mix_out_kernel, alpha=alpha),
        grid=(bsz, t // tm),
        in_specs=[
            row(d, 0),
            two(SSD_INNER, 0), two(SSD_INNER, 1),
            row(SSD_INNER, 0),
            row(SSD_INNER, COL_Z * LANES_V7X // SSD_INNER),
            row(DIFF_WIDTH, 0),
            two(RET_WIDTH, 0), two(RET_WIDTH, 1),
            row(RET_WIDTH, COL_RG * LANES_V7X // RET_WIDTH),
            vec(SSD_INNER), vec(SSD_INNER), vec(RET_V_DIM),
            pl.BlockSpec((MIX_WIDTH, d), lambda b, i: (0, 0)),
            pl.BlockSpec((None, 1, d), lambda b, i: (b, 0, 0)),
            vec(d), vec(d),
        ],
        out_specs=row(d, 0),
        out_shape=jax.ShapeDtypeStruct((bsz, t, d), f32),
        compiler_params=_cparams(("parallel", "parallel")),
        name="mix_out",
    )(x, y_ssd, y_ssd, xbc, proj, y_diff, y_ret, y_ret, proj,
      d_skip, ssd_norm_w.reshape(1, -1), ret_norm_w.reshape(1, -1), w_out, g_a, ln_w.reshape(1, -1), ln_b.reshape(1, -1))


def _gelu_exact(x):
    return 0.5 * x * (1.0 + lax.erf(x * (2.0 ** -0.5)))


FFN_SPLIT = 1


def _ffn_kernel(x_ref, xp_ref, xn_ref, sc_ref, sh_ref, gf_ref, wu_ref, wv_ref, cw_ref, cb_ref, wd_ref,
                lw_ref, lb_ref, o_ref, xe_ref, ue_ref, acc_ref, *, tm, nt, nf, alpha):
    i = pl.program_id(1)
    c = pl.program_id(2)
    h = SUBLANES_V7X
    hm = tm // FFN_SPLIT

    @pl.when(c == 0)
    def _():
        scale = 1.0 + sc_ref[...]
        shift = sh_ref[...]
        xe_ref[0:h, :] = jnp.where(i > 0, xp_ref[...] * scale + shift, 0.0)
        xe_ref[h:h + tm, :] = x_ref[...] * scale + shift
        xe_ref[h + tm:2 * h + tm, :] = jnp.where(i < nt - 1, xn_ref[...] * scale + shift, 0.0)
        acc_ref[...] = jnp.zeros(acc_ref.shape, f32)

    vs = []
    for r in range(FFN_SPLIT):
        r0 = r * hm
        ue_ref[r] = jnp.dot(xe_ref[r0:r0 + hm + 2 * h, :].astype(bf16), wu_ref[...], preferred_element_type=f32)
        vs.append(jnp.dot(xe_ref[h + r0:h + r0 + hm, :].astype(bf16), wv_ref[...], preferred_element_type=f32))
    p = FFN_CONV // 2
    for r in range(FFN_SPLIT):
        ue = ue_ref[r]
        n = ue.shape[0]
        u = cb_ref[...] + ue[h:h + hm] * cw_ref[p:p + 1, :]
        for j in range(FFN_CONV):
            if j != p:
                u = u + pltpu.roll(ue, (p - j) % n, 0)[h:h + hm] * cw_ref[j:j + 1, :]
        gated = (_gelu_exact(u) * vs[r]).astype(bf16)
        acc_ref[r * hm:(r + 1) * hm, :] += jnp.dot(gated, wd_ref[...], preferred_element_type=f32)

    @pl.when(c == nf - 1)
    def _():
        o_ref[...] = _layernorm(alpha * x_ref[...] + gf_ref[...] * acc_ref[...], lw_ref[...], lb_ref[...])


def _ffn_call(x, sc, sh, gf, w_up, conv_w, conv_b, w_down, ln_w, ln_b, alpha):
    bsz, t, d = x.shape
    tm = min(512, t)
    nt = t // tm
    tf = D_FF // 2
    nf = D_FF // tf
    hb = tm // SUBLANES_V7X
    last_hb = t // SUBLANES_V7X - 1
    mod = pl.BlockSpec((None, 1, d), lambda b, i, c: (b, 0, 0))
    vec = pl.BlockSpec((1, d), lambda b, i, c: (0, 0))
    return pl.pallas_call(
        functools.partial(_ffn_kernel, tm=tm, nt=nt, nf=nf, alpha=alpha),
        grid=(bsz, nt, nf),
        in_specs=[
            pl.BlockSpec((None, tm, d), lambda b, i, c: (b, i, 0)),
            pl.BlockSpec((None, SUBLANES_V7X, d), lambda b, i, c: (b, jnp.maximum(i * hb - 1, 0), 0)),
            pl.BlockSpec((None, SUBLANES_V7X, d), lambda b, i, c: (b, jnp.minimum((i + 1) * hb, last_hb), 0)),
            mod, mod, mod,
            pl.BlockSpec((d, tf), lambda b, i, c: (0, c)),
            pl.BlockSpec((d, tf), lambda b, i, c: (0, nf + c)),
            pl.BlockSpec((FFN_CONV, tf), lambda b, i, c: (0, c)),
            pl.BlockSpec((1, tf), lambda b, i, c: (0, c)),
            pl.BlockSpec((tf, d), lambda b, i, c: (c, 0)),
            vec, vec,
        ],
        out_specs=pl.BlockSpec((None, tm, d), lambda b, i, c: (b, i, 0)),
        out_shape=jax.ShapeDtypeStruct((bsz, t, d), f32),
        scratch_shapes=[
            pltpu.VMEM((tm + 2 * SUBLANES_V7X, d), f32),
            pltpu.VMEM((FFN_SPLIT, tm // FFN_SPLIT + 2 * SUBLANES_V7X, tf), f32),
            pltpu.VMEM((tm, d), f32),
        ],
        compiler_params=_cparams(("parallel", "parallel", "arbitrary")),
        name="conv_ffn",
    )(x, x, x, sc, sh, gf, w_up, w_up, conv_w, conv_b.reshape(1, -1), w_down, ln_w.reshape(1, -1), ln_b.reshape(1, -1))


def _ext_weight(w_in):
    d = w_in.shape[0]
    o = 0
    z = w_in[:, o:o + SSD_INNER]; o += SSD_INNER
    xbc = w_in[:, o:o + SSD_XBC]; o += SSD_XBC
    dt = w_in[:, o:o + 2 * SSD_HEADS]; o += 2 * SSD_HEADS
    dq = w_in[:, o:o + DIFF_QK]; o += DIFF_QK
    dk = w_in[:, o:o + DIFF_QK]; o += DIFF_QK
    dv = w_in[:, o:o + DIFF_WIDTH]; o += DIFF_WIDTH
    rq = w_in[:, o:o + RET_QK]; o += RET_QK
    rk = w_in[:, o:o + RET_QK]; o += RET_QK
    rv = w_in[:, o:o + RET_WIDTH]; o += RET_WIDTH
    rg = w_in[:, o:o + RET_WIDTH]; o += RET_WIDTH
    dt_pad = jnp.zeros((d, LANES_V7X - SSD_HEADS), w_in.dtype)
    main = jnp.concatenate([z, dq, dk, dv, rq, rk, rv, rg], axis=1)
    w_dt = jnp.concatenate([dt[:, :SSD_HEADS], dt_pad, dt[:, SSD_HEADS:], dt_pad], axis=1)
    return main.astype(bf16), xbc.astype(bf16), w_dt.astype(bf16)


def _rope_tables(s, n_ctx):
    rows = s // GRID_W
    row = jnp.repeat(jnp.arange(rows, dtype=f32), GRID_W)
    col = jnp.tile(jnp.arange(GRID_W, dtype=f32), rows)
    n_ax = DIFF_HEAD_DIM // 4
    inv_ax = 1.0 / (ROPE_BASE ** (jnp.arange(n_ax, dtype=f32) / n_ax))
    ar = row[:, None] * inv_ax
    ac = col[:, None] * inv_ax
    ang = jnp.concatenate([ar, ar, ac, ac], axis=-1)
    ang = jnp.tile(ang, (1, 2))
    inv_ret = 1.0 / (ROPE_BASE ** jnp.linspace(0.0, 1.0, RET_QK_DIM // 2, dtype=f32))
    ang_c = jnp.arange(n_ctx, dtype=f32)[:, None] * inv_ret
    ang_l = (n_ctx + jnp.arange(s, dtype=f32))[:, None] * inv_ret
    tile_ret = lambda a: jnp.tile(jnp.concatenate([a, a], axis=-1), (1, RET_HEADS))
    ang = jnp.tile(ang, (1, DIFF_HEADS))
    return {
        "diff_l": (jnp.cos(ang), jnp.sin(ang)),
        "diff_c": (jnp.ones((n_ctx, DIFF_WIDTH), f32), jnp.zeros((n_ctx, DIFF_WIDTH), f32)),
        "ret_l": (jnp.cos(tile_ret(ang_l)), jnp.sin(tile_ret(ang_l))),
        "ret_c": (jnp.cos(tile_ret(ang_c)), jnp.sin(tile_ret(ang_c))),
    }


def kernel(x, c, ctx, c_ctx, w_ada, b_ada, w_in, ssd_conv_w, ssd_conv_b, ssd_a_log, ssd_dt_bias, ssd_d, ssd_norm_w,
           diff_lambda, diff_norm_w, ret_decay, ret_norm_w, w_out, ln1_w, ln1_b, ffn_w_up, ffn_conv_w, ffn_conv_b,
           ffn_w_down, ln2_w, ln2_b):
    bsz, s, d = x.shape
    n_ctx = ctx.shape[1]
    depth = w_ada.shape[0]
    alpha = (2.0 * depth) ** 0.25
    assert d == D_MODEL and s % 512 == 0 and n_ctx % ATT_TP == 0 and s % GRID_W == 0

    tabs = _rope_tables(s, n_ctx)
    nrow = -(-(bsz + 1) // SUBLANES_V7X) * SUBLANES_V7X
    cvecs = jnp.zeros((nrow, d), f32).at[:bsz].set(c).at[bsz].set(c_ctx)
    mod_all = _ada_call(cvecs, w_ada, b_ada)

    gw = SSD_HPG * SSD_HEAD_DIM
    zero_ssd = jnp.zeros((bsz, 2, SSD_GROUPS, SSD_STATE, gw), f32)
    zero_ret = jnp.zeros((bsz, 2, RET_HEADS // 2, 2 * RET_QK_DIM, RET_V_DIM), f32)

    xc = ctx
    for li in range(depth):
        last = li == depth - 1
        lam_init = 0.8 - 0.6 * math.exp(-0.3 * li)
        mod = mod_all[li]
        m_lat = [mod[:bsz, k * d:(k + 1) * d].reshape(bsz, 1, d) for k in range(6)]
        m_ctx = [jnp.broadcast_to(mod[bsz, k * d:(k + 1) * d].reshape(1, 1, d), (bsz, 1, d)) for k in range(6)]
        w_main, w_xbc, w_dt = _ext_weight(w_in[li])
        w_out_b = w_out[li].astype(bf16)
        w_up_b = ffn_w_up[li].astype(bf16)
        w_down_b = ffn_w_down[li].astype(bf16)
        d_skip = jnp.repeat(ssd_d[li], SSD_HEAD_DIM).reshape(1, SSD_INNER)

        proj, xbc, dt_l = _in_proj_call(x, m_lat[1], m_lat[0], w_main, w_xbc, w_dt,
                                        ssd_conv_w[li], ssd_conv_b[li])
        proj_c, xbc_c, dt_c = _in_proj_call(xc, m_ctx[1], m_ctx[0], w_main, w_xbc, w_dt,
                                            ssd_conv_w[li], ssd_conv_b[li])

        ys_c, hs = _ssd_scan_call(xbc_c, dt_c, ssd_a_log[li], ssd_dt_bias[li], zero_ssd)
        ys, _ = _ssd_scan_call(xbc, dt_l, ssd_a_log[li], ssd_dt_bias[li], hs)

        yr_c, hr = _ret_scan_call(proj_c, *tabs["ret_c"], ret_decay[li], zero_ret)
        yr, _ = _ret_scan_call(proj, *tabs["ret_l"], ret_decay[li], hr)

        qt_c, k_c, vt_c = _attn_prep_call(proj_c, *tabs["diff_c"])
        qt_l, k_l, vt_l = _attn_prep_call(proj, *tabs["diff_l"])
        k_all = jnp.concatenate([k_l, k_c], axis=1)
        vt_all = jnp.concatenate([vt_l, vt_c], axis=3)
        y_diff = _attn_call(qt_l, k_all, vt_all, diff_lambda[li], diff_norm_w[li], lam_init)

        x = _mix_out_call(x, ys, xbc, proj, y_diff, yr, d_skip, ssd_norm_w[li], ret_norm_w[li], w_out_b,
                          m_lat[2], ln1_w[li], ln1_b[li], alpha)
        x = _ffn_call(x, m_lat[4], m_lat[3], m_lat[5], w_up_b, ffn_conv_w[li], ffn_conv_b[li], w_down_b,
                      ln2_w[li], ln2_b[li], alpha)
        if not last:
            yc_diff = _attn_call(qt_c, k_c, vt_c, diff_lambda[li], diff_norm_w[li], lam_init)
            xc = _mix_out_call(xc, ys_c, xbc_c, proj_c, yc_diff, yr_c, d_skip, ssd_norm_w[li], ret_norm_w[li], w_out_b,
                               m_ctx[2], ln1_w[li], ln1_b[li], alpha)
            xc = _ffn_call(xc, m_ctx[4], m_ctx[3], m_ctx[5], w_up_b, ffn_conv_w[li], ffn_conv_b[li], w_down_b,
                           ln2_w[li], ln2_b[li], alpha)
    return x
```

```python
import functools
import math

import jax
import jax.numpy as jnp
from jax import lax
from jax.experimental import pallas as pl
from jax.experimental.pallas import tpu as pltpu

f32 = jnp.float32
bf16 = jnp.bfloat16
HIGHEST = lax.Precision.HIGHEST

D_MODEL = 1024
GRID_W = 64
CHUNK = 128
ROPE_BASE = 10000.0
SSD_INNER = D_MODEL
SSD_HEAD_DIM = 64
SSD_HEADS = SSD_INNER // SSD_HEAD_DIM
SSD_GROUPS = 2
SSD_HPG = SSD_HEADS // SSD_GROUPS
SSD_STATE = 128
SSD_CONV = 5
SSD_XBC = SSD_INNER + 2 * SSD_GROUPS * SSD_STATE
SSD_COLS = SSD_INNER + SSD_XBC + 2 * SSD_HEADS
DIFF_WIDTH = D_MODEL // 2
DIFF_V_DIM = 128
DIFF_HEADS = DIFF_WIDTH // DIFF_V_DIM
DIFF_HEAD_DIM = DIFF_V_DIM // 2
DIFF_QK = DIFF_HEADS * 2 * DIFF_HEAD_DIM
DIFF_COLS = 2 * DIFF_QK + DIFF_WIDTH
RET_WIDTH = D_MODEL // 2
RET_V_DIM = 128
RET_HEADS = RET_WIDTH // RET_V_DIM
RET_QK_DIM = RET_V_DIM // 2
RET_QK = RET_HEADS * RET_QK_DIM
RET_COLS = 2 * RET_QK + 2 * RET_WIDTH
MIX_WIDTH = 2 * D_MODEL
D_FF = 11 * D_MODEL // 4
FFN_CONV = 3
EPS = 1e-5

LANES_V7X = 128
SUBLANES_V7X = 8
VMEM_LIMIT_V7X = 56 * 1024 * 1024

COL_Z = 0
COL_DQ = 8
COL_DK = 12
COL_DV = 16
COL_RQ = 20
COL_RK = 22
COL_RV = 24
COL_RG = 28
N_MAIN = 32 * LANES_V7X
NEG_BIG = -1e30


def _cparams(sem):
    return pltpu.CompilerParams(dimension_semantics=sem, vmem_limit_bytes=VMEM_LIMIT_V7X)


def _silu(x):
    return x * (1.0 / (1.0 + jnp.exp(-x)))


def _layernorm(x, w, b):
    mu = jnp.mean(x, axis=-1, keepdims=True)
    xc = x - mu
    var = jnp.mean(xc * xc, axis=-1, keepdims=True)
    return xc * lax.rsqrt(var + EPS) * w + b


def _rotate_half(x, half):
    n = x.shape[-1]
    lane = lax.broadcasted_iota(jnp.int32, x.shape, x.ndim - 1)
    first = (lane & (2 * half - 1)) < half
    return jnp.where(first, -pltpu.roll(x, n - half, x.ndim - 1), pltpu.roll(x, half, x.ndim - 1))


PIPE_BODY = 2


def _run_pipeline(step, n):
    n_loop = max((n - 1) // PIPE_BODY, 0)

    def body(k, carry):
        for u in range(PIPE_BODY):
            c = PIPE_BODY * k + u
            step(c, u % 2, c + 1)
        return carry

    lax.fori_loop(0, n_loop, body, 0)
    for c in range(PIPE_BODY * n_loop, n):
        step(c, c % 2, c + 1 if c + 1 < n else None)


def _split2(x):
    hi = x.astype(bf16)
    lo = (x - hi.astype(f32)).astype(bf16)
    return hi, lo


def _ada_kernel(c_ref, w_ref, b_ref, o_ref):
    c = c_ref[...]
    o_ref[...] = jnp.dot(_silu(c), w_ref[...], precision=HIGHEST, preferred_element_type=f32) + b_ref[...]


def _ada_call(cvecs, w_ada, b_ada):
    depth, d, n = w_ada.shape
    r = cvecs.shape[0]
    tn = 1536
    return pl.pallas_call(
        _ada_kernel,
        grid=(depth, n // tn),
        in_specs=[
            pl.BlockSpec((r, d), lambda l, j: (0, 0)),
            pl.BlockSpec((None, d, tn), lambda l, j: (l, 0, j)),
            pl.BlockSpec((None, 1, tn), lambda l, j: (l, 0, j)),
        ],
        out_specs=pl.BlockSpec((None, r, tn), lambda l, j: (l, 0, j)),
        out_shape=jax.ShapeDtypeStruct((depth, r, n), f32),
        compiler_params=_cparams(("parallel", "parallel")),
        name="ada",
    )(cvecs, w_ada, b_ada.reshape(depth, 1, n))


IN_PROJ_CW = 256
IN_PROJ_PIECE = 1024


def _in_proj_kernel(x_ref, xp_ref, xn_ref, sc_ref, sh_ref, w_ref, wx_ref, wdt_ref, cw_ref, cb_ref,
                    o_ref, oxbc_ref, odt_ref, xe_ref, *, tm, nt):
    i = pl.program_id(1)
    h = SUBLANES_V7X
    scale = 1.0 + sc_ref[...]
    shift = sh_ref[...]
    xe_ref[0:h, :] = xp_ref[...] * scale + shift
    xe_ref[h:h + tm, :] = x_ref[...] * scale + shift
    xe_ref[h + tm:2 * h + tm, :] = xn_ref[...] * scale + shift

    n = tm + 2 * h
    rowi = lax.broadcasted_iota(jnp.int32, (n, 1), 0)
    inside = ((rowi >= h) | (i > 0)) & ((rowi < h + tm) | (i < nt - 1))
    xe_all = xe_ref[...].astype(bf16)
    p = SSD_CONV // 2
    for c in range(SSD_XBC // IN_PROJ_CW):
        cols = slice(c * IN_PROJ_CW, (c + 1) * IN_PROJ_CW)
        ext = jnp.where(inside, jnp.dot(xe_all, wx_ref[:, cols], preferred_element_type=f32), 0.0)
        acc = cb_ref[:, cols] + ext[h:h + tm] * cw_ref[p:p + 1, cols]
        for j in range(SSD_CONV):
            if j != p:
                acc = acc + pltpu.roll(ext, (p - j) % n, 0)[h:h + tm] * cw_ref[j:j + 1, cols]
        oxbc_ref[:, cols] = _silu(acc).astype(oxbc_ref.dtype)

    xm = xe_ref[h:h + tm, :].astype(bf16)
    for c in range(w_ref.shape[1] // IN_PROJ_PIECE):
        cols = slice(c * IN_PROJ_PIECE, (c + 1) * IN_PROJ_PIECE)
        o_ref[:, cols] = jnp.dot(xm, w_ref[:, cols], preferred_element_type=f32).astype(o_ref.dtype)
    odt_ref[...] = jnp.dot(xm, wdt_ref[...], preferred_element_type=f32)


def _in_proj_call(x, sc, sh, w_main, w_xbc, w_dt, conv_w, conv_b):
    bsz, t, d = x.shape
    n = w_main.shape[1]
    ndt = w_dt.shape[1]
    tm = min(512, t)
    nt = t // tm
    hb = tm // SUBLANES_V7X
    last_hb = t // SUBLANES_V7X - 1
    mod = pl.BlockSpec((None, 1, d), lambda b, i: (b, 0, 0))
    whole = lambda a: pl.BlockSpec(a.shape, lambda b, i: (0,) * a.ndim)
    conv_b = conv_b.reshape(1, SSD_XBC)
    row = lambda width: pl.BlockSpec((None, tm, width), lambda b, i: (b, i, 0))
    return pl.pallas_call(
        functools.partial(_in_proj_kernel, tm=tm, nt=nt),
        grid=(bsz, nt),
        in_specs=[
            row(d),
            pl.BlockSpec((None, SUBLANES_V7X, d), lambda b, i: (b, jnp.maximum(i * hb - 1, 0), 0)),
            pl.BlockSpec((None, SUBLANES_V7X, d), lambda b, i: (b, jnp.minimum((i + 1) * hb, last_hb), 0)),
            mod, mod,
            whole(w_main), whole(w_xbc), whole(w_dt), whole(conv_w), whole(conv_b),
        ],
        out_specs=[row(n), row(SSD_XBC), row(ndt)],
        out_shape=[
            jax.ShapeDtypeStruct((bsz, t, n), bf16),
            jax.ShapeDtypeStruct((bsz, t, SSD_XBC), bf16),
            jax.ShapeDtypeStruct((bsz, t, ndt), f32),
        ],
        scratch_shapes=[pltpu.VMEM((tm + 2 * SUBLANES_V7X, d), f32)],
        compiler_params=_cparams(("parallel", "parallel")),
        name="in_proj",
    )(x, x, x, sc, sh, w_main, w_xbc, w_dt, conv_w, conv_b)


def _ssd_scan_kernel(cq_ref, bk_ref, x_ref, dt_ref, alog_ref, dtb_ref, tri_ref, e_ref, h0_ref,
                     y_ref, hout_ref, h_ref, *, ns, cps):
    d = pl.program_id(1)
    j = pl.program_id(2)
    L = CHUNK
    gw = SSD_HPG * SSD_HEAD_DIM

    @pl.when(j == 0)
    def _():
        h_ref[...] = h0_ref[...]

    tri = tri_ref[...]
    mask = tri > 0.5
    tri_b = tri.astype(bf16)
    a_neg = -jnp.exp(alog_ref[...])
    quad_w = 4 * SSD_HEAD_DIM
    lane = lax.broadcasted_iota(jnp.int32, (L, quad_w), 1)
    in_quarter = [(lane >= r * SSD_HEAD_DIM) & (lane < (r + 1) * SSD_HEAD_DIM) for r in range(4)]

    def one_chunk(rows):
        sp = jax.nn.softplus(dt_ref[rows, :] + dtb_ref[...])
        la = sp * a_neg
        la_hi = la.astype(bf16)
        rem = la - la_hi.astype(f32)
        la_mid = rem.astype(bf16)
        la_lo = (rem - la_mid.astype(f32)).astype(bf16)
        c3 = jnp.dot(tri_b, jnp.concatenate([la_hi, la_mid, la_lo], axis=1), preferred_element_type=f32)
        cum = c3[:, 0:LANES_V7X] + c3[:, LANES_V7X:2 * LANES_V7X] + c3[:, 2 * LANES_V7X:3 * LANES_V7X]
        cum_t = cum.T
        sp_t = sp.T
        tot = jnp.sum(la, axis=0, keepdims=True)
        w_end = jnp.exp(tot - cum) * sp
        e_cum = jnp.exp(cum)
        dec = jnp.broadcast_to(jnp.exp(tot), (SUBLANES_V7X, LANES_V7X))
        pieces = _split2(w_end) + _split2(e_cum) + _split2(dec)
        r = jnp.dot(jnp.concatenate(pieces, axis=0), e_ref[...], preferred_element_type=f32)
        w_x = r[0:L] + r[L:2 * L]
        ec_x = r[2 * L:3 * L] + r[3 * L:4 * L]
        dec_x = r[4 * L:4 * L + 1] + r[4 * L + SUBLANES_V7X:4 * L + SUBLANES_V7X + 1]

        x = x_ref[rows, :].astype(f32)
        xw = (x * w_x).astype(bf16)
        for g in range(SSD_GROUPS):
            qg = cq_ref[rows, g * SSD_STATE:(g + 1) * SSD_STATE]
            kg = bk_ref[rows, g * SSD_STATE:(g + 1) * SSD_STATE]
            scores = lax.dot_general(qg, kg, (((1,), (1,)), ((), ())), preferred_element_type=f32)
            kg_t = kg.astype(f32).T.astype(bf16)
            h_old = h_ref[g]
            y_off = jnp.dot(qg, h_old.astype(bf16), preferred_element_type=f32) * ec_x[:, g * gw:(g + 1) * gw]
            h_ref[g] = dec_x[:, g * gw:(g + 1) * gw] * h_old + jnp.dot(
                kg_t, xw[:, g * gw:(g + 1) * gw], preferred_element_type=f32)
            outs = []
            for qd in range(SSD_HPG // 4):
                ms = []
                for hh in range(4):
                    head = g * SSD_HPG + 4 * qd + hh
                    seg = cum[:, head:head + 1] - cum_t[head:head + 1, :]
                    decay = jnp.exp(jnp.where(mask, seg, NEG_BIG))
                    ms.append((scores * decay * sp_t[head:head + 1, :]).astype(bf16))
                c_lo = g * gw + qd * quad_w
                xq = x[:, c_lo:c_lo + quad_w]
                rhs = jnp.concatenate([jnp.where(in_quarter[r], xq, 0.0) for r in range(4)], axis=0).astype(bf16)
                outs.append(jnp.dot(jnp.concatenate(ms, axis=1), rhs, preferred_element_type=f32))
            y_ref[rows, g * gw:(g + 1) * gw] = (jnp.concatenate(outs, axis=1) + y_off).astype(y_ref.dtype)

    for t in range(cps):
        off = (t + d * (cps - 1 - 2 * t)) * L
        one_chunk(pl.ds(pl.multiple_of(off, L), L))

    @pl.when(j == ns - 1)
    def _():
        hout_ref[...] = h_ref[...]


def _chunks_per_step(nc):
    return 4 if nc % 4 == 0 else (2 if nc % 2 == 0 else 1)


def _ssd_scan_call(xbc, dt_raw, a_log, dt_bias, h0):
    bsz, t, _ = xbc.shape
    cps = _chunks_per_step(t // CHUNK)
    L = CHUNK
    rows = cps * L
    ns = t // rows
    gw = SSD_HPG * SSD_HEAD_DIM

    def chunk(d, j):
        return j + d * (ns - 1 - 2 * j)

    alog = jnp.zeros((2, 1, LANES_V7X), f32).at[:, 0, :SSD_HEADS].set(a_log)
    dtb = jnp.zeros((2, 1, LANES_V7X), f32).at[:, 0, :SSD_HEADS].set(dt_bias)
    idx = jnp.arange(L)
    tri = jnp.stack([(idx[None, :] <= idx[:, None]), (idx[None, :] >= idx[:, None])]).astype(f32)
    head_col = jnp.arange(LANES_V7X)[:, None]
    heads = (jnp.arange(SSD_INNER) // SSD_HEAD_DIM)[None, :]
    expand = (head_col == heads).astype(bf16)
    bcol = SSD_INNER // (SSD_GROUPS * SSD_STATE)
    return pl.pallas_call(
        functools.partial(_ssd_scan_kernel, ns=ns, cps=cps),
        grid=(bsz, 2, ns),
        in_specs=[
            pl.BlockSpec((None, rows, SSD_GROUPS * SSD_STATE), lambda b, d, j: (b, chunk(d, j), bcol + 1)),
            pl.BlockSpec((None, rows, SSD_GROUPS * SSD_STATE), lambda b, d, j: (b, chunk(d, j), bcol)),
            pl.BlockSpec((None, rows, SSD_INNER), lambda b, d, j: (b, chunk(d, j), 0)),
            pl.BlockSpec((None, rows, LANES_V7X), lambda b, d, j: (b, chunk(d, j), d)),
            pl.BlockSpec((None, 1, LANES_V7X), lambda b, d, j: (d, 0, 0)),
            pl.BlockSpec((None, 1, LANES_V7X), lambda b, d, j: (d, 0, 0)),
            pl.BlockSpec((None, L, L), lambda b, d, j: (d, 0, 0)),
            pl.BlockSpec((LANES_V7X, SSD_INNER), lambda b, d, j: (0, 0)),
            pl.BlockSpec((None, None, SSD_GROUPS, SSD_STATE, gw), lambda b, d, j: (b, d, 0, 0, 0)),
        ],
        out_specs=[
            pl.BlockSpec((None, None, rows, SSD_INNER), lambda b, d, j: (b, d, chunk(d, j), 0)),
            pl.BlockSpec((None, None, SSD_GROUPS, SSD_STATE, gw), lambda b, d, j: (b, d, 0, 0, 0)),
        ],
        out_shape=[
            jax.ShapeDtypeStruct((bsz, 2, t, SSD_INNER), bf16),
            jax.ShapeDtypeStruct((bsz, 2, SSD_GROUPS, SSD_STATE, gw), f32),
        ],
        scratch_shapes=[pltpu.VMEM((SSD_GROUPS, SSD_STATE, gw), f32)],
        compiler_params=_cparams(("parallel", "parallel", "arbitrary")),
        name="ssd_scan",
    )(xbc, xbc, xbc, dt_raw, alog, dtb, tri, expand, h0)


def _ret_scan_kernel(q_ref, k_ref, v_ref, cos_ref, sin_ref, rd_ref, h0_ref,
                     y_ref, hout_ref, h_ref, dec_ref, ecx_ref, tex_ref, dst_ref, *, ns, cps):
    d = pl.program_id(1)
    j = pl.program_id(2)
    L = CHUNK
    npair = RET_HEADS // 2

    @pl.when(j == 0)
    def _():
        h_ref[...] = h0_ref[...]
        lg = -jnp.exp(rd_ref[...])
        li = lax.broadcasted_iota(jnp.int32, (L, L), 0)
        si = lax.broadcasted_iota(jnp.int32, (L, L), 1)
        dist = jnp.where(d == 0, li - si, si - li)
        causal = dist >= 0
        distf = dist.astype(f32)
        lane = lax.broadcasted_iota(jnp.int32, (1, RET_WIDTH), 1)
        lgx = jnp.zeros((1, RET_WIDTH), f32)
        for h in range(RET_HEADS):
            lgh = lg[:, h:h + 1]
            dec_ref[h] = jnp.exp(jnp.where(causal, distf * lgh, NEG_BIG))
            lgx = jnp.where((lane >= h * RET_V_DIM) & (lane < (h + 1) * RET_V_DIM), lgh, lgx)
        pos = lax.broadcasted_iota(jnp.int32, (L, 1), 0)
        steps = jnp.where(d == 0, pos + 1, L - pos).astype(f32)
        cumx = steps * lgx
        totx = float(L) * lgx
        ecx_ref[...] = jnp.exp(cumx)
        tex_ref[...] = jnp.exp(totx - cumx)
        rowi = lax.broadcasted_iota(jnp.int32, (2 * RET_QK_DIM, RET_V_DIM), 0)
        for pp in range(npair):
            d0 = jnp.exp(float(L) * lg[:, 2 * pp:2 * pp + 1])
            d1 = jnp.exp(float(L) * lg[:, 2 * pp + 1:2 * pp + 2])
            dst_ref[pp] = jnp.where(rowi < RET_QK_DIM, d0, d1)

    ecx = ecx_ref[...]
    tex = tex_ref[...]
    hi_half = lax.broadcasted_iota(jnp.int32, (L, LANES_V7X), 1) >= RET_QK_DIM
    zeros_v = jnp.zeros((L, RET_V_DIM), bf16)

    def one_chunk(rows):
        cos = cos_ref[rows, :]
        sin = sin_ref[rows, :]
        q = q_ref[rows, :].astype(f32)
        k = k_ref[rows, :].astype(f32)
        q = q * cos + _rotate_half(q, RET_QK_DIM // 2) * sin
        k = (k * cos + _rotate_half(k, RET_QK_DIM // 2) * sin) * (RET_QK_DIM ** -0.5)
        vb = v_ref[rows, :]
        vw = (vb.astype(f32) * tex).astype(bf16)
        for pp in range(npair):
            qp = q[:, pp * LANES_V7X:(pp + 1) * LANES_V7X]
            kp = k[:, pp * LANES_V7X:(pp + 1) * LANES_V7X]
            kp_b = kp.astype(bf16)
            kp_t = kp.T.astype(bf16)
            r_old = h_ref[pp]
            r_b = r_old.astype(bf16)
            ms, offs = [], []
            for hh in range(2):
                h = 2 * pp + hh
                qm = (jnp.where(hi_half, qp, 0.0) if hh else jnp.where(hi_half, 0.0, qp)).astype(bf16)
                scores = lax.dot_general(qm, kp_b, (((1,), (1,)), ((), ())), preferred_element_type=f32)
                ms.append((scores * dec_ref[h]).astype(bf16))
                offs.append(jnp.dot(qm, r_b, preferred_element_type=f32))
            va = vb[:, (2 * pp) * RET_V_DIM:(2 * pp + 1) * RET_V_DIM]
            vc = vb[:, (2 * pp + 1) * RET_V_DIM:(2 * pp + 2) * RET_V_DIM]
            rhs = jnp.concatenate([jnp.concatenate([va, zeros_v], axis=1),
                                   jnp.concatenate([zeros_v, vc], axis=1)], axis=0)
            y_diag = jnp.dot(jnp.concatenate(ms, axis=1), rhs, preferred_element_type=f32)
            c0 = 2 * pp * RET_V_DIM
            y_ref[rows, c0:c0 + 2 * RET_V_DIM] = (
                y_diag + jnp.concatenate(offs, axis=1) * ecx[:, c0:c0 + 2 * RET_V_DIM]).astype(y_ref.dtype)
            upd = jnp.dot(kp_t, vw[:, c0:c0 + 2 * RET_V_DIM], preferred_element_type=f32)
            new = jnp.concatenate([upd[0:RET_QK_DIM, 0:RET_V_DIM],
                                   upd[RET_QK_DIM:2 * RET_QK_DIM, RET_V_DIM:2 * RET_V_DIM]], axis=0)
            h_ref[pp] = dst_ref[pp] * r_old + new

    for t in range(cps):
        off = (t + d * (cps - 1 - 2 * t)) * L
        one_chunk(pl.ds(pl.multiple_of(off, L), L))

    @pl.when(j == ns - 1)
    def _():
        hout_ref[...] = h_ref[...]


def _ret_scan_call(proj, cos, sin, decay_raw, h0):
    bsz, t, _ = proj.shape
    cps = _chunks_per_step(t // CHUNK)
    L = CHUNK
    rows = cps * L
    ns = t // rows
    npair = RET_HEADS // 2

    def chunk(d, j):
        return j + d * (ns - 1 - 2 * j)

    rd = jnp.zeros((2, 1, LANES_V7X), f32).at[:, 0, :RET_HEADS].set(decay_raw)
    qk = lambda c: pl.BlockSpec((None, rows, RET_QK), lambda b, d, j: (b, chunk(d, j), c // 2))
    tab = pl.BlockSpec((rows, RET_QK), lambda b, d, j: (chunk(d, j), 0))
    st = pl.BlockSpec((None, None, npair, 2 * RET_QK_DIM, RET_V_DIM), lambda b, d, j: (b, d, 0, 0, 0))
    return pl.pallas_call(
        functools.partial(_ret_scan_kernel, ns=ns, cps=cps),
        grid=(bsz, 2, ns),
        in_specs=[
            qk(COL_RQ), qk(COL_RK),
            pl.BlockSpec((None, rows, RET_WIDTH), lambda b, d, j: (b, chunk(d, j), COL_RV // 4)),
            tab, tab,
            pl.BlockSpec((None, 1, LANES_V7X), lambda b, d, j: (d, 0, 0)),
            st,
        ],
        out_specs=[
            pl.BlockSpec((None, None, rows, RET_WIDTH), lambda b, d, j: (b, d, chunk(d, j), 0)),
            st,
        ],
        out_shape=[
            jax.ShapeDtypeStruct((bsz, 2, t, RET_WIDTH), bf16),
            jax.ShapeDtypeStruct((bsz, 2, npair, 2 * RET_QK_DIM, RET_V_DIM), f32),
        ],
        scratch_shapes=[
            pltpu.VMEM((npair, 2 * RET_QK_DIM, RET_V_DIM), f32),
            pltpu.VMEM((RET_HEADS, L, L), f32),
            pltpu.VMEM((L, RET_WIDTH), f32),
            pltpu.VMEM((L, RET_WIDTH), f32),
            pltpu.VMEM((npair, 2 * RET_QK_DIM, RET_V_DIM), f32),
        ],
        compiler_params=_cparams(("parallel", "parallel", "arbitrary")),
        name="ret_scan",
    )(proj, proj, proj, cos, sin, rd, h0)


ATT_TP = 256
ATT_TQS = 256
ATT_VROWS = DIFF_V_DIM + 16
LOG2E = 1.4426950408889634


def _attn_prep_kernel(q_ref, k_ref, v_ref, cos_ref, sin_ref, qt_ref, ko_ref, vt_ref):
    cos = cos_ref[...]
    sin = sin_ref[...]
    ax = DIFF_HEAD_DIM // 4
    q = q_ref[...].astype(f32)
    k = k_ref[...].astype(f32)
    q = (q * cos + _rotate_half(q, ax) * sin) * (DIFF_HEAD_DIM ** -0.5 * LOG2E)
    ko_ref[...] = (k * cos + _rotate_half(k, ax) * sin).astype(bf16)
    v = v_ref[...].astype(f32)
    ones = jnp.ones((ATT_VROWS - DIFF_V_DIM, vt_ref.shape[2]), bf16)
    for h in range(DIFF_HEADS):
        cols = slice(h * LANES_V7X, (h + 1) * LANES_V7X)
        qt_ref[h] = q[:, cols].T.astype(bf16)
        vt_ref[h, 0:DIFF_V_DIM, :] = v[:, cols].T.astype(bf16)
        vt_ref[h, DIFF_V_DIM:ATT_VROWS, :] = ones


def _attn_prep_call(proj, cos, sin):
    bsz, t, _ = proj.shape
    tp = ATT_TP
    nt = t // tp
    w = DIFF_WIDTH
    col = lambda c0: pl.BlockSpec((None, tp, w), lambda b, i: (b, i, c0 * LANES_V7X // w))
    tab = pl.BlockSpec((tp, w), lambda b, i: (i, 0))
    return pl.pallas_call(
        _attn_prep_kernel,
        grid=(bsz, nt),
        in_specs=[col(COL_DQ), col(COL_DK), col(COL_DV), tab, tab],
        out_specs=[
            pl.BlockSpec((None, DIFF_HEADS, LANES_V7X, tp), lambda b, i: (b, 0, 0, i)),
            pl.BlockSpec((None, tp, w), lambda b, i: (b, i, 0)),
            pl.BlockSpec((None, DIFF_HEADS, ATT_VROWS, tp), lambda b, i: (b, 0, 0, i)),
        ],
        out_shape=[
            jax.ShapeDtypeStruct((bsz, DIFF_HEADS, 2 * DIFF_HEAD_DIM, t), bf16),
            jax.ShapeDtypeStruct((bsz, t, DIFF_WIDTH), bf16),
            jax.ShapeDtypeStruct((bsz, DIFF_HEADS, ATT_VROWS, t), bf16),
        ],
        compiler_params=_cparams(("parallel", "parallel")),
        name="attn_prep",
    )(proj, proj, proj, cos, sin)


def _attn_kernel(qt_ref, k_ref, vt_ref, lam_ref, nw_ref, o_ref, q2_ref, m_ref, acc_ref, s_ref, mx_ref,
                 *, nk, tk, lam_init):
    nsub, tqs = q2_ref.shape[1], q2_ref.shape[3]
    row = lax.broadcasted_iota(jnp.int32, (2 * DIFF_HEAD_DIM, tqs), 0)
    for sb in range(nsub):
        qt = qt_ref[:, sb * tqs:(sb + 1) * tqs]
        q2_ref[0, sb] = jnp.where(row < DIFF_HEAD_DIM, qt, jnp.zeros_like(qt))
        q2_ref[1, sb] = jnp.where(row < DIFF_HEAD_DIM, jnp.zeros_like(qt), qt)
    m_ref[...] = jnp.full(m_ref.shape, NEG_BIG, f32)
    acc_ref[...] = jnp.zeros(acc_ref.shape, f32)
    chains = [(mi, sb) for mi in range(2) for sb in range(nsub)]

    def key_chunk(c):
        return k_ref[pl.ds(pl.multiple_of(c * tk, ATT_TP), tk), :]

    def value_chunk(c):
        return vt_ref[:, pl.ds(pl.multiple_of(c * tk, ATT_TP), tk)]

    def step(c, par, c_next):
        oth = 1 - par
        kn = None if c_next is None else key_chunk(c_next)
        vt = value_chunk(c)
        for mi, sb in chains:
            if kn is not None:
                scores(kn, oth, mi, sb)
            m_old = m_ref[mi, sb]
            m_new = jnp.maximum(m_old, mx_ref[par, mi, sb])
            p = jnp.exp2(s_ref[par, mi, sb] - m_new).astype(bf16)
            acc_ref[mi, sb] = (jnp.exp2(m_old - m_new) * acc_ref[mi, sb]
                               + jnp.dot(vt, p, preferred_element_type=f32))
            m_ref[mi, sb] = m_new

    def scores(kc, slot, mi, sb):
        s = jnp.dot(kc, q2_ref[mi, sb], preferred_element_type=f32)
        s_ref[slot, mi, sb] = s
        mx_ref[slot, mi, sb] = jnp.max(s, axis=0, keepdims=True)

    k0 = key_chunk(0)
    for mi, sb in chains:
        scores(k0, 0, mi, sb)
    _run_pipeline(step, nk)

    lp = lam_ref[...]
    lam = (jnp.exp(jnp.sum(lp[0:1] * lp[1:2], axis=1, keepdims=True))
           - jnp.exp(jnp.sum(lp[2:3] * lp[3:4], axis=1, keepdims=True)) + lam_init)
    nv = DIFF_V_DIM
    for sb in range(nsub):
        ot = (acc_ref[0, sb, 0:nv, :] / acc_ref[0, sb, nv:nv + 1, :]
              - lam * (acc_ref[1, sb, 0:nv, :] / acc_ref[1, sb, nv:nv + 1, :]))
        o = ot.T
        ms = jnp.mean(o * o, axis=-1, keepdims=True)
        o_ref[sb * tqs:(sb + 1) * tqs, :] = (
            o * lax.rsqrt(ms + EPS) * nw_ref[...] * (1.0 - lam_init)).astype(o_ref.dtype)


def _attn_call(qt, k_all, vt_all, lam_p, norm_w, lam_init):
    bsz, nh, _, sq = qt.shape
    sk = k_all.shape[1]
    tk = 3 * ATT_TP if sk % (3 * ATT_TP) == 0 else ATT_TP
    nk = sk // tk
    tq = min(1024, sq)
    tqs = min(ATT_TQS, tq)
    nsub = tq // tqs
    return pl.pallas_call(
        functools.partial(_attn_kernel, nk=nk, tk=tk, lam_init=lam_init),
        grid=(bsz, nh, sq // tq),
        in_specs=[
            pl.BlockSpec((None, None, 2 * DIFF_HEAD_DIM, tq), lambda b, h, i: (b, h, 0, i)),
            pl.BlockSpec((None, sk, LANES_V7X), lambda b, h, i: (b, 0, h)),
            pl.BlockSpec((None, None, ATT_VROWS, sk), lambda b, h, i: (b, h, 0, 0)),
            pl.BlockSpec((4, DIFF_HEAD_DIM), lambda b, h, i: (0, 0)),
            pl.BlockSpec((1, DIFF_V_DIM), lambda b, h, i: (0, 0)),
        ],
        out_specs=pl.BlockSpec((None, tq, DIFF_V_DIM), lambda b, h, i: (b, i, h)),
        out_shape=jax.ShapeDtypeStruct((bsz, sq, DIFF_WIDTH), bf16),
        scratch_shapes=[
            pltpu.VMEM((2, nsub, 2 * DIFF_HEAD_DIM, tqs), bf16),
            pltpu.VMEM((2, nsub, 1, tqs), f32),
            pltpu.VMEM((2, nsub, ATT_VROWS, tqs), f32),
            pltpu.VMEM((2, 2, nsub, tk, tqs), f32),
            pltpu.VMEM((2, 2, nsub, 1, tqs), f32),
        ],
        compiler_params=_cparams(("parallel", "parallel", "parallel")),
        name="diff_attn",
    )(qt, k_all, vt_all, lam_p, norm_w.reshape(1, DIFF_V_DIM))


def _mix_out_kernel(x_ref, ysf_ref, ysb_ref, xs_ref, z_ref, yd_ref, yrf_ref, yrb_ref, g_ref,
                    dsk_ref, snw_ref, rnw_ref, wo_ref, ga_ref, lw_ref, lb_ref, o_ref, *, alpha):
    y = ysf_ref[...].astype(f32) + ysb_ref[...].astype(f32) + xs_ref[...].astype(f32) * dsk_ref[...]
    y = y * _silu(z_ref[...].astype(f32))
    y_ssd = y * lax.rsqrt(jnp.mean(y * y, axis=-1, keepdims=True) + EPS) * snw_ref[...]
    yr = yrf_ref[...].astype(f32) + yrb_ref[...].astype(f32)
    gate = _silu(g_ref[...].astype(f32))
    rets = []
    for h in range(RET_HEADS):
        yh = yr[:, h * RET_V_DIM:(h + 1) * RET_V_DIM]
        mu = jnp.mean(yh, axis=-1, keepdims=True)
        yc = yh - mu
        var = jnp.mean(yc * yc, axis=-1, keepdims=True)
        rets.append(yc * lax.rsqrt(var + EPS) * rnw_ref[...])
    y_ret = jnp.concatenate(rets, axis=1) * gate
    ycat = jnp.concatenate([y_ssd.astype(bf16), yd_ref[...], y_ret.astype(bf16)], axis=1)
    mixed = jnp.dot(ycat, wo_ref[...], preferred_element_type=f32)
    o_ref[...] = _layernorm(alpha * x_ref[...] + ga_ref[...] * mixed, lw_ref[...], lb_ref[...])


def _mix_out_call(x, y_ssd, xbc, proj, y_diff, y_ret, d_skip, ssd_norm_w, ret_norm_w, w_out, g_a, ln_w, ln_b, alpha):
    bsz, t, d = x.shape
    tm = min(512, t)
    row = lambda width, c: pl.BlockSpec((None, tm, width), lambda b, i: (b, i, c))
    two = lambda width, dd: pl.BlockSpec((None, None, tm, width), lambda b, i: (b, dd, i, 0))
    vec = lambda width: pl.BlockSpec((1, width), lambda b, i: (0, 0))
    return pl.pallas_call(
        functools.partial(_mix_out_kernel, alpha=alpha),
        grid=(bsz, t // tm),
        in_specs=[
            row(d, 0),
            two(SSD_INNER, 0), two(SSD_INNER, 1),
            row(SSD_INNER, 0),
            row(SSD_INNER, COL_Z * LANES_V7X // SSD_INNER),
            row(DIFF_WIDTH, 0),
            two(RET_WIDTH, 0), two(RET_WIDTH, 1),
            row(RET_WIDTH, COL_RG * LANES_V7X // RET_WIDTH),
            vec(SSD_INNER), vec(SSD_INNER), vec(RET_V_DIM),
            pl.BlockSpec((MIX_WIDTH, d), lambda b, i: (0, 0)),
            pl.BlockSpec((None, 1, d), lambda b, i: (b, 0, 0)),
            vec(d), vec(d),
        ],
        out_specs=row(d, 0),
        out_shape=jax.ShapeDtypeStruct((bsz, t, d), f32),
        compiler_params=_cparams(("parallel", "parallel")),
        name="mix_out",
    )(x, y_ssd, y_ssd, xbc, proj, y_diff, y_ret, y_ret, proj,
      d_skip, ssd_norm_w.reshape(1, -1), ret_norm_w.reshape(1, -1), w_out, g_a, ln_w.reshape(1, -1), ln_b.reshape(1, -1))


def _gelu_exact(x):
    return 0.5 * x * (1.0 + lax.erf(x * (2.0 ** -0.5)))


FFN_SPLIT = 1


def _ffn_kernel(x_ref, xp_ref, xn_ref, sc_ref, sh_ref, gf_ref, wu_ref, wv_ref, cw_ref, cb_ref, wd_ref,
                lw_ref, lb_ref, o_ref, xe_ref, ue_ref, acc_ref, *, tm, nt, nf, alpha):
    i = pl.program_id(1)
    c = pl.program_id(2)
    h = SUBLANES_V7X
    hm = tm // FFN_SPLIT

    @pl.when(c == 0)
    def _():
        scale = 1.0 + sc_ref[...]
        shift = sh_ref[...]
        xe_ref[0:h, :] = jnp.where(i > 0, xp_ref[...] * scale + shift, 0.0)
        xe_ref[h:h + tm, :] = x_ref[...] * scale + shift
        xe_ref[h + tm:2 * h + tm, :] = jnp.where(i < nt - 1, xn_ref[...] * scale + shift, 0.0)
        acc_ref[...] = jnp.zeros(acc_ref.shape, f32)

    vs = []
    for r in range(FFN_SPLIT):
        r0 = r * hm
        ue_ref[r] = jnp.dot(xe_ref[r0:r0 + hm + 2 * h, :].astype(bf16), wu_ref[...], preferred_element_type=f32)
        vs.append(jnp.dot(xe_ref[h + r0:h + r0 + hm, :].astype(bf16), wv_ref[...], preferred_element_type=f32))
    p = FFN_CONV // 2
    for r in range(FFN_SPLIT):
        ue = ue_ref[r]
        n = ue.shape[0]
        u = cb_ref[...] + ue[h:h + hm] * cw_ref[p:p + 1, :]
        for j in range(FFN_CONV):
            if j != p:
                u = u + pltpu.roll(ue, (p - j) % n, 0)[h:h + hm] * cw_ref[j:j + 1, :]
        gated = (_gelu_exact(u) * vs[r]).astype(bf16)
        acc_ref[r * hm:(r + 1) * hm, :] += jnp.dot(gated, wd_ref[...], preferred_element_type=f32)

    @pl.when(c == nf - 1)
    def _():
        o_ref[...] = _layernorm(alpha * x_ref[...] + gf_ref[...] * acc_ref[...], lw_ref[...], lb_ref[...])


def _ffn_call(x, sc, sh, gf, w_up, conv_w, conv_b, w_down, ln_w, ln_b, alpha):
    bsz, t, d = x.shape
    tm = min(512, t)
    nt = t // tm
    tf = D_FF
    nf = D_FF // tf
    hb = tm // SUBLANES_V7X
    last_hb = t // SUBLANES_V7X - 1
    mod = pl.BlockSpec((None, 1, d), lambda b, i, c: (b, 0, 0))
    vec = pl.BlockSpec((1, d), lambda b, i, c: (0, 0))
    resident = dict(pipeline_mode=pl.Buffered(1)) if nf == 1 else {}
    return pl.pallas_call(
        functools.partial(_ffn_kernel, tm=tm, nt=nt, nf=nf, alpha=alpha),
        grid=(bsz, nt, nf),
        in_specs=[
            pl.BlockSpec((None, tm, d), lambda b, i, c: (b, i, 0)),
            pl.BlockSpec((None, SUBLANES_V7X, d), lambda b, i, c: (b, jnp.maximum(i * hb - 1, 0), 0)),
            pl.BlockSpec((None, SUBLANES_V7X, d), lambda b, i, c: (b, jnp.minimum((i + 1) * hb, last_hb), 0)),
            mod, mod, mod,
            pl.BlockSpec((d, tf), lambda b, i, c: (0, c), **resident),
            pl.BlockSpec((d, tf), lambda b, i, c: (0, nf + c), **resident),
            pl.BlockSpec((FFN_CONV, tf), lambda b, i, c: (0, c)),
            pl.BlockSpec((1, tf), lambda b, i, c: (0, c)),
            pl.BlockSpec((tf, d), lambda b, i, c: (c, 0), **resident),
            vec, vec,
        ],
        out_specs=pl.BlockSpec((None, tm, d), lambda b, i, c: (b, i, 0)),
        out_shape=jax.ShapeDtypeStruct((bsz, t, d), f32),
        scratch_shapes=[
            pltpu.VMEM((tm + 2 * SUBLANES_V7X, d), f32),
            pltpu.VMEM((FFN_SPLIT, tm // FFN_SPLIT + 2 * SUBLANES_V7X, tf), f32),
            pltpu.VMEM((tm, d), f32),
        ],
        compiler_params=_cparams(("parallel", "parallel", "arbitrary")),
        name="conv_ffn",
    )(x, x, x, sc, sh, gf, w_up, w_up, conv_w, conv_b.reshape(1, -1), w_down, ln_w.reshape(1, -1), ln_b.reshape(1, -1))


def _ext_weight(w_in):
    d = w_in.shape[0]
    o = 0
    z = w_in[:, o:o + SSD_INNER]; o += SSD_INNER
    xbc = w_in[:, o:o + SSD_XBC]; o += SSD_XBC
    dt = w_in[:, o:o + 2 * SSD_HEADS]; o += 2 * SSD_HEADS
    dq = w_in[:, o:o + DIFF_QK]; o += DIFF_QK
    dk = w_in[:, o:o + DIFF_QK]; o += DIFF_QK
    dv = w_in[:, o:o + DIFF_WIDTH]; o += DIFF_WIDTH
    rq = w_in[:, o:o + RET_QK]; o += RET_QK
    rk = w_in[:, o:o + RET_QK]; o += RET_QK
    rv = w_in[:, o:o + RET_WIDTH]; o += RET_WIDTH
    rg = w_in[:, o:o + RET_WIDTH]; o += RET_WIDTH
    dt_pad = jnp.zeros((d, LANES_V7X - SSD_HEADS), w_in.dtype)
    main = jnp.concatenate([z, dq, dk, dv, rq, rk, rv, rg], axis=1)
    w_dt = jnp.concatenate([dt[:, :SSD_HEADS], dt_pad, dt[:, SSD_HEADS:], dt_pad], axis=1)
    return main.astype(bf16), xbc.astype(bf16), w_dt.astype(bf16)


def _rope_tables(s, n_ctx):
    rows = s // GRID_W
    row = jnp.repeat(jnp.arange(rows, dtype=f32), GRID_W)
    col = jnp.tile(jnp.arange(GRID_W, dtype=f32), rows)
    n_ax = DIFF_HEAD_DIM // 4
    inv_ax = 1.0 / (ROPE_BASE ** (jnp.arange(n_ax, dtype=f32) / n_ax))
    ar = row[:, None] * inv_ax
    ac = col[:, None] * inv_ax
    ang = jnp.concatenate([ar, ar, ac, ac], axis=-1)
    ang = jnp.tile(ang, (1, 2))
    inv_ret = 1.0 / (ROPE_BASE ** jnp.linspace(0.0, 1.0, RET_QK_DIM // 2, dtype=f32))
    ang_c = jnp.arange(n_ctx, dtype=f32)[:, None] * inv_ret
    ang_l = (n_ctx + jnp.arange(s, dtype=f32))[:, None] * inv_ret
    tile_ret = lambda a: jnp.tile(jnp.concatenate([a, a], axis=-1), (1, RET_HEADS))
    ang = jnp.tile(ang, (1, DIFF_HEADS))
    return {
        "diff_l": (jnp.cos(ang), jnp.sin(ang)),
        "diff_c": (jnp.ones((n_ctx, DIFF_WIDTH), f32), jnp.zeros((n_ctx, DIFF_WIDTH), f32)),
        "ret_l": (jnp.cos(tile_ret(ang_l)), jnp.sin(tile_ret(ang_l))),
        "ret_c": (jnp.cos(tile_ret(ang_c)), jnp.sin(tile_ret(ang_c))),
    }


def kernel(x, c, ctx, c_ctx, w_ada, b_ada, w_in, ssd_conv_w, ssd_conv_b, ssd_a_log, ssd_dt_bias, ssd_d, ssd_norm_w,
           diff_lambda, diff_norm_w, ret_decay, ret_norm_w, w_out, ln1_w, ln1_b, ffn_w_up, ffn_conv_w, ffn_conv_b,
           ffn_w_down, ln2_w, ln2_b):
    bsz, s, d = x.shape
    n_ctx = ctx.shape[1]
    depth = w_ada.shape[0]
    alpha = (2.0 * depth) ** 0.25
    assert d == D_MODEL and s % 512 == 0 and n_ctx % ATT_TP == 0 and s % GRID_W == 0

    tabs = _rope_tables(s, n_ctx)
    nrow = -(-(bsz + 1) // SUBLANES_V7X) * SUBLANES_V7X
    cvecs = jnp.zeros((nrow, d), f32).at[:bsz].set(c).at[bsz].set(c_ctx)
    mod_all = _ada_call(cvecs, w_ada, b_ada)

    gw = SSD_HPG * SSD_HEAD_DIM
    zero_ssd = jnp.zeros((bsz, 2, SSD_GROUPS, SSD_STATE, gw), f32)
    zero_ret = jnp.zeros((bsz, 2, RET_HEADS // 2, 2 * RET_QK_DIM, RET_V_DIM), f32)

    xc = ctx
    for li in range(depth):
        last = li == depth - 1
        lam_init = 0.8 - 0.6 * math.exp(-0.3 * li)
        mod = mod_all[li]
        m_lat = [mod[:bsz, k * d:(k + 1) * d].reshape(bsz, 1, d) for k in range(6)]
        m_ctx = [jnp.broadcast_to(mod[bsz, k * d:(k + 1) * d].reshape(1, 1, d), (bsz, 1, d)) for k in range(6)]
        w_main, w_xbc, w_dt = _ext_weight(w_in[li])
        w_out_b = w_out[li].astype(bf16)
        w_up_b = ffn_w_up[li].astype(bf16)
        w_down_b = ffn_w_down[li].astype(bf16)
        d_skip = jnp.repeat(ssd_d[li], SSD_HEAD_DIM).reshape(1, SSD_INNER)

        proj, xbc, dt_l = _in_proj_call(x, m_lat[1], m_lat[0], w_main, w_xbc, w_dt,
                                        ssd_conv_w[li], ssd_conv_b[li])
        proj_c, xbc_c, dt_c = _in_proj_call(xc, m_ctx[1], m_ctx[0], w_main, w_xbc, w_dt,
                                            ssd_conv_w[li], ssd_conv_b[li])

        ys_c, hs = _ssd_scan_call(xbc_c, dt_c, ssd_a_log[li], ssd_dt_bias[li], zero_ssd)
        ys, _ = _ssd_scan_call(xbc, dt_l, ssd_a_log[li], ssd_dt_bias[li], hs)

        yr_c, hr = _ret_scan_call(proj_c, *tabs["ret_c"], ret_decay[li], zero_ret)
        yr, _ = _ret_scan_call(proj, *tabs["ret_l"], ret_decay[li], hr)

        qt_c, k_c, vt_c = _attn_prep_call(proj_c, *tabs["diff_c"])
        qt_l, k_l, vt_l = _attn_prep_call(proj, *tabs["diff_l"])
        k_all = jnp.concatenate([k_l, k_c], axis=1)
        vt_all = jnp.concatenate([vt_l, vt_c], axis=3)
        y_diff = _attn_call(qt_l, k_all, vt_all, diff_lambda[li], diff_norm_w[li], lam_init)

        x = _mix_out_call(x, ys, xbc, proj, y_diff, yr, d_skip, ssd_norm_w[li], ret_norm_w[li], w_out_b,
                          m_lat[2], ln1_w[li], ln1_b[li], alpha)
        x = _ffn_call(x, m_lat[4], m_lat[3], m_lat[5], w_up_b, ffn_conv_w[li], ffn_conv_b[li], w_down_b,
                      ln2_w[li], ln2_b[li], alpha)
        if not last:
            yc_diff = _attn_call(qt_c, k_c, vt_c, diff_lambda[li], diff_norm_w[li], lam_init)
            xc = _mix_out_call(xc, ys_c, xbc_c, proj_c, yc_diff, yr_c, d_skip, ssd_norm_w[li], ret_norm_w[li], w_out_b,
                               m_ctx[2], ln1_w[li], ln1_b[li], alpha)
            xc = _ffn_call(xc, m_ctx[4], m_ctx[3], m_ctx[5], w_up_b, ffn_conv_w[li], ffn_conv_b[li], w_down_b,
                           ln2_w[li], ln2_b[li], alpha)
    return x
```

```python
import functools
import math

import jax
import jax.numpy as jnp
from jax import lax
from jax.experimental import pallas as pl
from jax.experimental.pallas import tpu as pltpu

f32 = jnp.float32
bf16 = jnp.bfloat16
HIGHEST = lax.Precision.HIGHEST

D_MODEL = 1024
GRID_W = 64
CHUNK = 128
ROPE_BASE = 10000.0
SSD_INNER = D_MODEL
SSD_HEAD_DIM = 64
SSD_HEADS = SSD_INNER // SSD_HEAD_DIM
SSD_GROUPS = 2
SSD_HPG = SSD_HEADS // SSD_GROUPS
SSD_STATE = 128
SSD_CONV = 5
SSD_XBC = SSD_INNER + 2 * SSD_GROUPS * SSD_STATE
SSD_COLS = SSD_INNER + SSD_XBC + 2 * SSD_HEADS
DIFF_WIDTH = D_MODEL // 2
DIFF_V_DIM = 128
DIFF_HEADS = DIFF_WIDTH // DIFF_V_DIM
DIFF_HEAD_DIM = DIFF_V_DIM // 2
DIFF_QK = DIFF_HEADS * 2 * DIFF_HEAD_DIM
DIFF_COLS = 2 * DIFF_QK + DIFF_WIDTH
RET_WIDTH = D_MODEL // 2
RET_V_DIM = 128
RET_HEADS = RET_WIDTH // RET_V_DIM
RET_QK_DIM = RET_V_DIM // 2
RET_QK = RET_HEADS * RET_QK_DIM
RET_COLS = 2 * RET_QK + 2 * RET_WIDTH
MIX_WIDTH = 2 * D_MODEL
D_FF = 11 * D_MODEL // 4
FFN_CONV = 3
EPS = 1e-5

LANES_V7X = 128
SUBLANES_V7X = 8
VMEM_LIMIT_V7X = 56 * 1024 * 1024

COL_Z = 0
COL_DQ = 8
COL_DK = 12
COL_DV = 16
COL_RQ = 20
COL_RK = 22
COL_RV = 24
COL_RG = 28
N_MAIN = 32 * LANES_V7X
NEG_BIG = -1e30


def _cparams(sem):
    return pltpu.CompilerParams(dimension_semantics=sem, vmem_limit_bytes=VMEM_LIMIT_V7X)


def _silu(x):
    return x * (1.0 / (1.0 + jnp.exp(-x)))


def _layernorm(x, w, b):
    mu = jnp.mean(x, axis=-1, keepdims=True)
    xc = x - mu
    var = jnp.mean(xc * xc, axis=-1, keepdims=True)
    return xc * lax.rsqrt(var + EPS) * w + b


def _rotate_half(x, half):
    n = x.shape[-1]
    lane = lax.broadcasted_iota(jnp.int32, x.shape, x.ndim - 1)
    first = (lane & (2 * half - 1)) < half
    return jnp.where(first, -pltpu.roll(x, n - half, x.ndim - 1), pltpu.roll(x, half, x.ndim - 1))


PIPE_BODY = 2


def _run_pipeline(step, n):
    n_loop = max((n - 2) // PIPE_BODY, 0)

    def body(k, carry):
        for u in range(PIPE_BODY):
            c = PIPE_BODY * k + u
            step(c, u % 2, c + 1)
        return carry

    if n_loop:
        lax.fori_loop(0, n_loop, body, 0)
    for c in range(PIPE_BODY * n_loop, n):
        step(c, c % 2, c + 1 if c + 1 < n else None)


def _split2(x):
    hi = x.astype(bf16)
    lo = (x - hi.astype(f32)).astype(bf16)
    return hi, lo


def _ada_kernel(c_ref, w_ref, b_ref, o_ref):
    c = c_ref[...]
    o_ref[...] = jnp.dot(_silu(c), w_ref[...], precision=HIGHEST, preferred_element_type=f32) + b_ref[...]


def _ada_call(cvecs, w_ada, b_ada):
    depth, d, n = w_ada.shape
    r = cvecs.shape[0]
    tn = 1536
    return pl.pallas_call(
        _ada_kernel,
        grid=(depth, n // tn),
        in_specs=[
            pl.BlockSpec((r, d), lambda l, j: (0, 0)),
            pl.BlockSpec((None, d, tn), lambda l, j: (l, 0, j)),
            pl.BlockSpec((None, 1, tn), lambda l, j: (l, 0, j)),
        ],
        out_specs=pl.BlockSpec((None, r, tn), lambda l, j: (l, 0, j)),
        out_shape=jax.ShapeDtypeStruct((depth, r, n), f32),
        compiler_params=_cparams(("parallel", "parallel")),
        name="ada",
    )(cvecs, w_ada, b_ada.reshape(depth, 1, n))


IN_PROJ_CW = 256
IN_PROJ_PIECE = 1024


def _in_proj_kernel(x_ref, xp_ref, xn_ref, sc_ref, sh_ref, w_ref, wx_ref, wdt_ref, cw_ref, cb_ref,
                    o_ref, oxbc_ref, odt_ref, xe_ref, pre_ref, *, tm, nt):
    i = pl.program_id(1)
    h = SUBLANES_V7X
    scale = 1.0 + sc_ref[...]
    shift = sh_ref[...]
    xe_ref[0:h, :] = xp_ref[...] * scale + shift
    xe_ref[h:h + tm, :] = x_ref[...] * scale + shift
    xe_ref[h + tm:2 * h + tm, :] = xn_ref[...] * scale + shift

    n = tm + 2 * h
    rowi = lax.broadcasted_iota(jnp.int32, (n, 1), 0)
    inside = ((rowi >= h) | (i > 0)) & ((rowi < h + tm) | (i < nt - 1))
    p = SSD_CONV // 2
    pre_ref[...] = jnp.dot(xe_ref[...].astype(bf16), wx_ref[...], preferred_element_type=f32)
    xm = xe_ref[h:h + tm, :].astype(bf16)
    o_ref[...] = jnp.dot(xm, w_ref[...], preferred_element_type=f32).astype(o_ref.dtype)
    odt_ref[...] = jnp.dot(xm, wdt_ref[...], preferred_element_type=f32)
    for c in range(SSD_XBC // IN_PROJ_CW):
        cols = slice(c * IN_PROJ_CW, (c + 1) * IN_PROJ_CW)
        ext = jnp.where(inside, pre_ref[:, cols], 0.0)
        acc = cb_ref[:, cols] + ext[h:h + tm] * cw_ref[p:p + 1, cols]
        for j in range(SSD_CONV):
            if j != p:
                acc = acc + pltpu.roll(ext, (p - j) % n, 0)[h:h + tm] * cw_ref[j:j + 1, cols]
        oxbc_ref[:, cols] = _silu(acc).astype(oxbc_ref.dtype)


def _in_proj_call(x, sc, sh, w_main, w_xbc, w_dt, conv_w, conv_b):
    bsz, t, d = x.shape
    n = w_main.shape[1]
    ndt = w_dt.shape[1]
    tm = min(512, t)
    nt = t // tm
    hb = tm // SUBLANES_V7X
    last_hb = t // SUBLANES_V7X - 1
    mod = pl.BlockSpec((None, 1, d), lambda b, i: (b, 0, 0))
    whole = lambda a: pl.BlockSpec(a.shape, lambda b, i: (0,) * a.ndim, pipeline_mode=pl.Buffered(1))
    conv_b = conv_b.reshape(1, SSD_XBC)
    row = lambda width: pl.BlockSpec((None, tm, width), lambda b, i: (b, i, 0))
    return pl.pallas_call(
        functools.partial(_in_proj_kernel, tm=tm, nt=nt),
        grid=(bsz, nt),
        in_specs=[
            row(d),
            pl.BlockSpec((None, SUBLANES_V7X, d), lambda b, i: (b, jnp.maximum(i * hb - 1, 0), 0)),
            pl.BlockSpec((None, SUBLANES_V7X, d), lambda b, i: (b, jnp.minimum((i + 1) * hb, last_hb), 0)),
            mod, mod,
            whole(w_main), whole(w_xbc), whole(w_dt), whole(conv_w), whole(conv_b),
        ],
        out_specs=[row(n), row(SSD_XBC), row(ndt)],
        out_shape=[
            jax.ShapeDtypeStruct((bsz, t, n), bf16),
            jax.ShapeDtypeStruct((bsz, t, SSD_XBC), bf16),
            jax.ShapeDtypeStruct((bsz, t, ndt), f32),
        ],
        scratch_shapes=[pltpu.VMEM((tm + 2 * SUBLANES_V7X, d), f32),
                        pltpu.VMEM((tm + 2 * SUBLANES_V7X, SSD_XBC), f32)],
        compiler_params=_cparams(("parallel", "parallel")),
        name="in_proj",
    )(x, x, x, sc, sh, w_main, w_xbc, w_dt, conv_w, conv_b)


def _ssd_scan_kernel(cq_ref, bk_ref, x_ref, dt_ref, alog_ref, dtb_ref, tri_ref, e_ref, h0_ref,
                     y_ref, hout_ref, h_ref, *, ns, cps):
    d = pl.program_id(1)
    j = pl.program_id(2)
    L = CHUNK
    gw = SSD_HPG * SSD_HEAD_DIM

    @pl.when(j == 0)
    def _():
        h_ref[...] = h0_ref[...]

    tri = tri_ref[...]
    mask = tri > 0.5
    tri_b = tri.astype(bf16)
    a_neg = -jnp.exp(alog_ref[...])
    quad_w = 4 * SSD_HEAD_DIM
    lane = lax.broadcasted_iota(jnp.int32, (L, quad_w), 1)
    in_quarter = [(lane >= r * SSD_HEAD_DIM) & (lane < (r + 1) * SSD_HEAD_DIM) for r in range(4)]

    def one_chunk(rows):
        sp = jax.nn.softplus(dt_ref[rows, :] + dtb_ref[...])
        la = sp * a_neg
        la_hi = la.astype(bf16)
        rem = la - la_hi.astype(f32)
        la_mid = rem.astype(bf16)
        la_lo = (rem - la_mid.astype(f32)).astype(bf16)
        c3 = jnp.dot(tri_b, jnp.concatenate([la_hi, la_mid, la_lo], axis=1), preferred_element_type=f32)
        cum = c3[:, 0:LANES_V7X] + c3[:, LANES_V7X:2 * LANES_V7X] + c3[:, 2 * LANES_V7X:3 * LANES_V7X]
        cum_t = cum.T
        sp_t = sp.T
        tot = jnp.sum(la, axis=0, keepdims=True)
        w_end = jnp.exp(tot - cum) * sp
        e_cum = jnp.exp(cum)
        dec = jnp.broadcast_to(jnp.exp(tot), (SUBLANES_V7X, LANES_V7X))
        pieces = _split2(w_end) + _split2(e_cum) + _split2(dec)
        r = jnp.dot(jnp.concatenate(pieces, axis=0), e_ref[...], preferred_element_type=f32)
        w_x = r[0:L] + r[L:2 * L]
        ec_x = r[2 * L:3 * L] + r[3 * L:4 * L]
        dec_x = r[4 * L:4 * L + 1] + r[4 * L + SUBLANES_V7X:4 * L + SUBLANES_V7X + 1]

        x = x_ref[rows, :].astype(f32)
        xw = (x * w_x).astype(bf16)
        for g in range(SSD_GROUPS):
            qg = cq_ref[rows, g * SSD_STATE:(g + 1) * SSD_STATE]
            kg = bk_ref[rows, g * SSD_STATE:(g + 1) * SSD_STATE]
            scores = lax.dot_general(qg, kg, (((1,), (1,)), ((), ())), preferred_element_type=f32)
            kg_t = kg.astype(f32).T.astype(bf16)
            h_old = h_ref[g]
            y_off = jnp.dot(qg, h_old.astype(bf16), preferred_element_type=f32) * ec_x[:, g * gw:(g + 1) * gw]
            h_ref[g] = dec_x[:, g * gw:(g + 1) * gw] * h_old + jnp.dot(
                kg_t, xw[:, g * gw:(g + 1) * gw], preferred_element_type=f32)
            outs = []
            for qd in range(SSD_HPG // 4):
                ms = []
                for hh in range(4):
                    head = g * SSD_HPG + 4 * qd + hh
                    seg = cum[:, head:head + 1] - cum_t[head:head + 1, :]
                    decay = jnp.exp(jnp.where(mask, seg, NEG_BIG))
                    ms.append((scores * decay * sp_t[head:head + 1, :]).astype(bf16))
                c_lo = g * gw + qd * quad_w
                xq = x[:, c_lo:c_lo + quad_w]
                rhs = jnp.concatenate([jnp.where(in_quarter[r], xq, 0.0) for r in range(4)], axis=0).astype(bf16)
                outs.append(jnp.dot(jnp.concatenate(ms, axis=1), rhs, preferred_element_type=f32))
            y_ref[rows, g * gw:(g + 1) * gw] = (jnp.concatenate(outs, axis=1) + y_off).astype(y_ref.dtype)

    for t in range(cps):
        off = (t + d * (cps - 1 - 2 * t)) * L
        one_chunk(pl.ds(pl.multiple_of(off, L), L))

    @pl.when(j == ns - 1)
    def _():
        hout_ref[...] = h_ref[...]


def _chunks_per_step(nc):
    return 4 if nc % 4 == 0 else (2 if nc % 2 == 0 else 1)


def _ssd_scan_call(xbc, dt_raw, a_log, dt_bias, h0):
    bsz, t, _ = xbc.shape
    cps = _chunks_per_step(t // CHUNK)
    L = CHUNK
    rows = cps * L
    ns = t // rows
    gw = SSD_HPG * SSD_HEAD_DIM

    def chunk(d, j):
        return j + d * (ns - 1 - 2 * j)

    alog = jnp.zeros((2, 1, LANES_V7X), f32).at[:, 0, :SSD_HEADS].set(a_log)
    dtb = jnp.zeros((2, 1, LANES_V7X), f32).at[:, 0, :SSD_HEADS].set(dt_bias)
    idx = jnp.arange(L)
    tri = jnp.stack([(idx[None, :] <= idx[:, None]), (idx[None, :] >= idx[:, None])]).astype(f32)
    head_col = jnp.arange(LANES_V7X)[:, None]
    heads = (jnp.arange(SSD_INNER) // SSD_HEAD_DIM)[None, :]
    expand = (head_col == heads).astype(bf16)
    bcol = SSD_INNER // (SSD_GROUPS * SSD_STATE)
    return pl.pallas_call(
        functools.partial(_ssd_scan_kernel, ns=ns, cps=cps),
        grid=(bsz, 2, ns),
        in_specs=[
            pl.BlockSpec((None, rows, SSD_GROUPS * SSD_STATE), lambda b, d, j: (b, chunk(d, j), bcol + 1)),
            pl.BlockSpec((None, rows, SSD_GROUPS * SSD_STATE), lambda b, d, j: (b, chunk(d, j), bcol)),
            pl.BlockSpec((None, rows, SSD_INNER), lambda b, d, j: (b, chunk(d, j), 0)),
            pl.BlockSpec((None, rows, LANES_V7X), lambda b, d, j: (b, chunk(d, j), d)),
            pl.BlockSpec((None, 1, LANES_V7X), lambda b, d, j: (d, 0, 0)),
            pl.BlockSpec((None, 1, LANES_V7X), lambda b, d, j: (d, 0, 0)),
            pl.BlockSpec((None, L, L), lambda b, d, j: (d, 0, 0)),
            pl.BlockSpec((LANES_V7X, SSD_INNER), lambda b, d, j: (0, 0)),
            pl.BlockSpec((None, None, SSD_GROUPS, SSD_STATE, gw), lambda b, d, j: (b, d, 0, 0, 0)),
        ],
        out_specs=[
            pl.BlockSpec((None, None, rows, SSD_INNER), lambda b, d, j: (b, d, chunk(d, j), 0)),
            pl.BlockSpec((None, None, SSD_GROUPS, SSD_STATE, gw), lambda b, d, j: (b, d, 0, 0, 0)),
        ],
        out_shape=[
            jax.ShapeDtypeStruct((bsz, 2, t, SSD_INNER), bf16),
            jax.ShapeDtypeStruct((bsz, 2, SSD_GROUPS, SSD_STATE, gw), f32),
        ],
        scratch_shapes=[pltpu.VMEM((SSD_GROUPS, SSD_STATE, gw), f32)],
        compiler_params=_cparams(("parallel", "parallel", "arbitrary")),
        name="ssd_scan",
    )(xbc, xbc, xbc, dt_raw, alog, dtb, tri, expand, h0)


def _ret_scan_kernel(q_ref, k_ref, v_ref, cos_ref, sin_ref, rd_ref, h0_ref,
                     y_ref, hout_ref, h_ref, dec_ref, ecx_ref, tex_ref, dst_ref, *, ns, cps):
    d = pl.program_id(1)
    j = pl.program_id(2)
    L = CHUNK
    npair = RET_HEADS // 2

    @pl.when(j == 0)
    def _():
        h_ref[...] = h0_ref[...]
        lg = -jnp.exp(rd_ref[...])
        li = lax.broadcasted_iota(jnp.int32, (L, L), 0)
        si = lax.broadcasted_iota(jnp.int32, (L, L), 1)
        dist = jnp.where(d == 0, li - si, si - li)
        causal = dist >= 0
        distf = dist.astype(f32)
        lane = lax.broadcasted_iota(jnp.int32, (1, RET_WIDTH), 1)
        lgx = jnp.zeros((1, RET_WIDTH), f32)
        for h in range(RET_HEADS):
            lgh = lg[:, h:h + 1]
            dec_ref[h] = jnp.exp(jnp.where(causal, distf * lgh, NEG_BIG))
            lgx = jnp.where((lane >= h * RET_V_DIM) & (lane < (h + 1) * RET_V_DIM), lgh, lgx)
        pos = lax.broadcasted_iota(jnp.int32, (L, 1), 0)
        steps = jnp.where(d == 0, pos + 1, L - pos).astype(f32)
        cumx = steps * lgx
        totx = float(L) * lgx
        ecx_ref[...] = jnp.exp(cumx)
        tex_ref[...] = jnp.exp(totx - cumx)
        rowi = lax.broadcasted_iota(jnp.int32, (2 * RET_QK_DIM, RET_V_DIM), 0)
        for pp in range(npair):
            d0 = jnp.exp(float(L) * lg[:, 2 * pp:2 * pp + 1])
            d1 = jnp.exp(float(L) * lg[:, 2 * pp + 1:2 * pp + 2])
            dst_ref[pp] = jnp.where(rowi < RET_QK_DIM, d0, d1)

    ecx = ecx_ref[...]
    tex = tex_ref[...]
    hi_half = lax.broadcasted_iota(jnp.int32, (L, LANES_V7X), 1) >= RET_QK_DIM
    zeros_v = jnp.zeros((L, RET_V_DIM), bf16)

    def one_chunk(rows):
        cos = cos_ref[rows, :]
        sin = sin_ref[rows, :]
        q = q_ref[rows, :].astype(f32)
        k = k_ref[rows, :].astype(f32)
        q = q * cos + _rotate_half(q, RET_QK_DIM // 2) * sin
        k = (k * cos + _rotate_half(k, RET_QK_DIM // 2) * sin) * (RET_QK_DIM ** -0.5)
        vb = v_ref[rows, :]
        vw = (vb.astype(f32) * tex).astype(bf16)
        for pp in range(npair):
            qp = q[:, pp * LANES_V7X:(pp + 1) * LANES_V7X]
            kp = k[:, pp * LANES_V7X:(pp + 1) * LANES_V7X]
            kp_b = kp.astype(bf16)
            kp_t = kp.T.astype(bf16)
            r_old = h_ref[pp]
            r_b = r_old.astype(bf16)
            ms, offs = [], []
            for hh in range(2):
                h = 2 * pp + hh
                qm = (jnp.where(hi_half, qp, 0.0) if hh else jnp.where(hi_half, 0.0, qp)).astype(bf16)
                scores = lax.dot_general(qm, kp_b, (((1,), (1,)), ((), ())), preferred_element_type=f32)
                ms.append((scores * dec_ref[h]).astype(bf16))
                offs.append(jnp.dot(qm, r_b, preferred_element_type=f32))
            va = vb[:, (2 * pp) * RET_V_DIM:(2 * pp + 1) * RET_V_DIM]
            vc = vb[:, (2 * pp + 1) * RET_V_DIM:(2 * pp + 2) * RET_V_DIM]
            rhs = jnp.concatenate([jnp.concatenate([va, zeros_v], axis=1),
                                   jnp.concatenate([zeros_v, vc], axis=1)], axis=0)
            y_diag = jnp.dot(jnp.concatenate(ms, axis=1), rhs, preferred_element_type=f32)
            c0 = 2 * pp * RET_V_DIM
            y_ref[rows, c0:c0 + 2 * RET_V_DIM] = (
                y_diag + jnp.concatenate(offs, axis=1) * ecx[:, c0:c0 + 2 * RET_V_DIM]).astype(y_ref.dtype)
            upd = jnp.dot(kp_t, vw[:, c0:c0 + 2 * RET_V_DIM], preferred_element_type=f32)
            new = jnp.concatenate([upd[0:RET_QK_DIM, 0:RET_V_DIM],
                                   upd[RET_QK_DIM:2 * RET_QK_DIM, RET_V_DIM:2 * RET_V_DIM]], axis=0)
            h_ref[pp] = dst_ref[pp] * r_old + new

    for t in range(cps):
        off = (t + d * (cps - 1 - 2 * t)) * L
        one_chunk(pl.ds(pl.multiple_of(off, L), L))

    @pl.when(j == ns - 1)
    def _():
        hout_ref[...] = h_ref[...]


def _ret_scan_call(proj, cos, sin, decay_raw, h0):
    bsz, t, _ = proj.shape
    cps = _chunks_per_step(t // CHUNK)
    L = CHUNK
    rows = cps * L
    ns = t // rows
    npair = RET_HEADS // 2

    def chunk(d, j):
        return j + d * (ns - 1 - 2 * j)

    rd = jnp.zeros((2, 1, LANES_V7X), f32).at[:, 0, :RET_HEADS].set(decay_raw)
    qk = lambda c: pl.BlockSpec((None, rows, RET_QK), lambda b, d, j: (b, chunk(d, j), c // 2))
    tab = pl.BlockSpec((rows, RET_QK), lambda b, d, j: (chunk(d, j), 0))
    st = pl.BlockSpec((None, None, npair, 2 * RET_QK_DIM, RET_V_DIM), lambda b, d, j: (b, d, 0, 0, 0))
    return pl.pallas_call(
        functools.partial(_ret_scan_kernel, ns=ns, cps=cps),
        grid=(bsz, 2, ns),
        in_specs=[
            qk(COL_RQ), qk(COL_RK),
            pl.BlockSpec((None, rows, RET_WIDTH), lambda b, d, j: (b, chunk(d, j), COL_RV // 4)),
            tab, tab,
            pl.BlockSpec((None, 1, LANES_V7X), lambda b, d, j: (d, 0, 0)),
            st,
        ],
        out_specs=[
            pl.BlockSpec((None, None, rows, RET_WIDTH), lambda b, d, j: (b, d, chunk(d, j), 0)),
            st,
        ],
        out_shape=[
            jax.ShapeDtypeStruct((bsz, 2, t, RET_WIDTH), bf16),
            jax.ShapeDtypeStruct((bsz, 2, npair, 2 * RET_QK_DIM, RET_V_DIM), f32),
        ],
        scratch_shapes=[
            pltpu.VMEM((npair, 2 * RET_QK_DIM, RET_V_DIM), f32),
            pltpu.VMEM((RET_HEADS, L, L), f32),
            pltpu.VMEM((L, RET_WIDTH), f32),
            pltpu.VMEM((L, RET_WIDTH), f32),
            pltpu.VMEM((npair, 2 * RET_QK_DIM, RET_V_DIM), f32),
        ],
        compiler_params=_cparams(("parallel", "parallel", "arbitrary")),
        name="ret_scan",
    )(proj, proj, proj, cos, sin, rd, h0)


ATT_TP = 256
ATT_TQS = 256
ATT_VROWS = DIFF_V_DIM + 16
LOG2E = 1.4426950408889634


def _attn_prep_kernel(q_ref, k_ref, v_ref, cos_ref, sin_ref, qt_ref, ko_ref, vt_ref):
    cos = cos_ref[...]
    sin = sin_ref[...]
    ax = DIFF_HEAD_DIM // 4
    q = q_ref[...].astype(f32)
    k = k_ref[...].astype(f32)
    q = (q * cos + _rotate_half(q, ax) * sin) * (DIFF_HEAD_DIM ** -0.5 * LOG2E)
    ko_ref[...] = (k * cos + _rotate_half(k, ax) * sin).astype(bf16)
    v = v_ref[...].astype(f32)
    ones = jnp.ones((ATT_VROWS - DIFF_V_DIM, vt_ref.shape[2]), bf16)
    for h in range(DIFF_HEADS):
        cols = slice(h * LANES_V7X, (h + 1) * LANES_V7X)
        qt_ref[h] = q[:, cols].T.astype(bf16)
        vt_ref[h, 0:DIFF_V_DIM, :] = v[:, cols].T.astype(bf16)
        vt_ref[h, DIFF_V_DIM:ATT_VROWS, :] = ones


def _attn_prep_call(proj, cos, sin):
    bsz, t, _ = proj.shape
    tp = ATT_TP
    nt = t // tp
    w = DIFF_WIDTH
    col = lambda c0: pl.BlockSpec((None, tp, w), lambda b, i: (b, i, c0 * LANES_V7X // w))
    tab = pl.BlockSpec((tp, w), lambda b, i: (i, 0))
    return pl.pallas_call(
        _attn_prep_kernel,
        grid=(bsz, nt),
        in_specs=[col(COL_DQ), col(COL_DK), col(COL_DV), tab, tab],
        out_specs=[
            pl.BlockSpec((None, DIFF_HEADS, LANES_V7X, tp), lambda b, i: (b, 0, 0, i)),
            pl.BlockSpec((None, tp, w), lambda b, i: (b, i, 0)),
            pl.BlockSpec((None, DIFF_HEADS, ATT_VROWS, tp), lambda b, i: (b, 0, 0, i)),
        ],
        out_shape=[
            jax.ShapeDtypeStruct((bsz, DIFF_HEADS, 2 * DIFF_HEAD_DIM, t), bf16),
            jax.ShapeDtypeStruct((bsz, t, DIFF_WIDTH), bf16),
            jax.ShapeDtypeStruct((bsz, DIFF_HEADS, ATT_VROWS, t), bf16),
        ],
        compiler_params=_cparams(("parallel", "parallel")),
        name="attn_prep",
    )(proj, proj, proj, cos, sin)


def _attn_kernel(*refs, n_lat, nk, tk, lam_init):
    if n_lat:
        qt_ref, kl_ref, vl_ref, kc_ref, vc_ref, lam_ref, nw_ref, o_ref, q2_ref, m_ref, acc_ref, s_ref, mx_ref = refs
    else:
        qt_ref, kc_ref, vc_ref, lam_ref, nw_ref, o_ref, q2_ref, m_ref, acc_ref, s_ref, mx_ref = refs
    lat_full = (nk - 1) * tk
    nsub, tqs = q2_ref.shape[1], q2_ref.shape[3]
    row = lax.broadcasted_iota(jnp.int32, (2 * DIFF_HEAD_DIM, tqs), 0)
    for sb in range(nsub):
        qt = qt_ref[:, sb * tqs:(sb + 1) * tqs]
        q2_ref[0, sb] = jnp.where(row < DIFF_HEAD_DIM, qt, jnp.zeros_like(qt))
        q2_ref[1, sb] = jnp.where(row < DIFF_HEAD_DIM, jnp.zeros_like(qt), qt)
    m_ref[...] = jnp.full(m_ref.shape, NEG_BIG, f32)
    acc_ref[...] = jnp.zeros(acc_ref.shape, f32)
    chains = [(mi, sb) for mi in range(2) for sb in range(nsub)]

    def is_last(c):
        return isinstance(c, int) and c == nk - 1

    def key_chunk(c):
        if is_last(c):
            parts = ([kl_ref[lat_full:n_lat, :]] if n_lat > lat_full else []) + [kc_ref[...]]
            return parts[0] if len(parts) == 1 else jnp.concatenate(parts, axis=0)
        return kl_ref[pl.ds(pl.multiple_of(c * tk, ATT_TP), tk), :]

    def value_chunk(c):
        if is_last(c):
            parts = ([vl_ref[:, lat_full:n_lat]] if n_lat > lat_full else []) + [vc_ref[...]]
            return parts[0] if len(parts) == 1 else jnp.concatenate(parts, axis=1)
        return vl_ref[:, pl.ds(pl.multiple_of(c * tk, ATT_TP), tk)]

    def step(c, par, c_next):
        oth = 1 - par
        kn = None if c_next is None else key_chunk(c_next)
        vt = value_chunk(c)
        for mi, sb in chains:
            if kn is not None:
                scores(kn, oth, mi, sb)
            m_old = m_ref[mi, sb]
            m_new = jnp.maximum(m_old, mx_ref[par, mi, sb])
            p = jnp.exp2(s_ref[par, mi, sb] - m_new).astype(bf16)
            acc_ref[mi, sb] = (jnp.exp2(m_old - m_new) * acc_ref[mi, sb]
                               + jnp.dot(vt, p, preferred_element_type=f32))
            m_ref[mi, sb] = m_new

    def scores(kc, slot, mi, sb):
        s = jnp.dot(kc, q2_ref[mi, sb], preferred_element_type=f32)
        s_ref[slot, mi, sb] = s
        mx_ref[slot, mi, sb] = jnp.max(s, axis=0, keepdims=True)

    k0 = key_chunk(0)
    for mi, sb in chains:
        scores(k0, 0, mi, sb)
    _run_pipeline(step, nk)

    lp = lam_ref[...]
    lam = (jnp.exp(jnp.sum(lp[0:1] * lp[1:2], axis=1, keepdims=True))
           - jnp.exp(jnp.sum(lp[2:3] * lp[3:4], axis=1, keepdims=True)) + lam_init)
    nv = DIFF_V_DIM
    for sb in range(nsub):
        ot = (acc_ref[0, sb, 0:nv, :] / acc_ref[0, sb, nv:nv + 1, :]
              - lam * (acc_ref[1, sb, 0:nv, :] / acc_ref[1, sb, nv:nv + 1, :]))
        o = ot.T
        ms = jnp.mean(o * o, axis=-1, keepdims=True)
        o_ref[sb * tqs:(sb + 1) * tqs, :] = (
            o * lax.rsqrt(ms + EPS) * nw_ref[...] * (1.0 - lam_init)).astype(o_ref.dtype)


def _attn_call(qt, lat, ctx, lam_p, norm_w, lam_init):
    bsz, nh, _, sq = qt.shape
    n_lat = lat[0].shape[1] if lat is not None else 0
    n_ctx = ctx[0].shape[1]
    sk = n_lat + n_ctx
    tk = 3 * ATT_TP if sk % (3 * ATT_TP) == 0 else ATT_TP
    nk = sk // tk
    assert (nk - 1) * tk <= n_lat
    tq = min(1024, sq)
    tqs = min(ATT_TQS, tq)
    nsub = tq // tqs
    kv_specs, kv_args = [], []
    for src in ([lat] if lat is not None else []) + [ctx]:
        n = src[0].shape[1]
        kv_specs += [pl.BlockSpec((None, n, LANES_V7X), lambda b, h, i: (b, 0, h)),
                     pl.BlockSpec((None, None, ATT_VROWS, n), lambda b, h, i: (b, h, 0, 0))]
        kv_args += list(src)
    return pl.pallas_call(
        functools.partial(_attn_kernel, n_lat=n_lat, nk=nk, tk=tk, lam_init=lam_init),
        grid=(bsz, nh, sq // tq),
        in_specs=[
            pl.BlockSpec((None, None, 2 * DIFF_HEAD_DIM, tq), lambda b, h, i: (b, h, 0, i)),
            *kv_specs,
            pl.BlockSpec((4, DIFF_HEAD_DIM), lambda b, h, i: (0, 0)),
            pl.BlockSpec((1, DIFF_V_DIM), lambda b, h, i: (0, 0)),
        ],
        out_specs=pl.BlockSpec((None, tq, DIFF_V_DIM), lambda b, h, i: (b, i, h)),
        out_shape=jax.ShapeDtypeStruct((bsz, sq, DIFF_WIDTH), bf16),
        scratch_shapes=[
            pltpu.VMEM((2, nsub, 2 * DIFF_HEAD_DIM, tqs), bf16),
            pltpu.VMEM((2, nsub, 1, tqs), f32),
            pltpu.VMEM((2, nsub, ATT_VROWS, tqs), f32),
            pltpu.VMEM((2, 2, nsub, tk, tqs), f32),
            pltpu.VMEM((2, 2, nsub, 1, tqs), f32),
        ],
        compiler_params=_cparams(("parallel", "parallel", "parallel")),
        name="diff_attn",
    )(qt, *kv_args, lam_p, norm_w.reshape(1, DIFF_V_DIM))


def _mix_out_kernel(x_ref, ysf_ref, ysb_ref, xs_ref, z_ref, yd_ref, yrf_ref, yrb_ref, g_ref,
                    dsk_ref, snw_ref, rnw_ref, wo_ref, ga_ref, lw_ref, lb_ref, o_ref, *, alpha):
    y = ysf_ref[...].astype(f32) + ysb_ref[...].astype(f32) + xs_ref[...].astype(f32) * dsk_ref[...]
    y = y * _silu(z_ref[...].astype(f32))
    y_ssd = y * lax.rsqrt(jnp.mean(y * y, axis=-1, keepdims=True) + EPS) * snw_ref[...]
    yr = yrf_ref[...].astype(f32) + yrb_ref[...].astype(f32)
    gate = _silu(g_ref[...].astype(f32))
    rets = []
    for h in range(RET_HEADS):
        yh = yr[:, h * RET_V_DIM:(h + 1) * RET_V_DIM]
        mu = jnp.mean(yh, axis=-1, keepdims=True)
        yc = yh - mu
        var = jnp.mean(yc * yc, axis=-1, keepdims=True)
        rets.append(yc * lax.rsqrt(var + EPS) * rnw_ref[...])
    y_ret = jnp.concatenate(rets, axis=1) * gate
    ycat = jnp.concatenate([y_ssd.astype(bf16), yd_ref[...], y_ret.astype(bf16)], axis=1)
    mixed = jnp.dot(ycat, wo_ref[...], preferred_element_type=f32)
    o_ref[...] = _layernorm(alpha * x_ref[...] + ga_ref[...] * mixed, lw_ref[...], lb_ref[...])


def _mix_out_call(x, y_ssd, xbc, proj, y_diff, y_ret, d_skip, ssd_norm_w, ret_norm_w, w_out, g_a, ln_w, ln_b, alpha):
    bsz, t, d = x.shape
    tm = min(512, t)
    row = lambda width, c: pl.BlockSpec((None, tm, width), lambda b, i: (b, i, c))
    two = lambda width, dd: pl.BlockSpec((None, None, tm, width), lambda b, i: (b, dd, i, 0))
    vec = lambda width: pl.BlockSpec((1, width), lambda b, i: (0, 0))
    return pl.pallas_call(
        functools.partial(_mix_out_kernel, alpha=alpha),
        grid=(bsz, t // tm),
        in_specs=[
            row(d, 0),
            two(SSD_INNER, 0), two(SSD_INNER, 1),
            row(SSD_INNER, 0),
            row(SSD_INNER, COL_Z * LANES_V7X // SSD_INNER),
            row(DIFF_WIDTH, 0),
            two(RET_WIDTH, 0), two(RET_WIDTH, 1),
            row(RET_WIDTH, COL_RG * LANES_V7X // RET_WIDTH),
            vec(SSD_INNER), vec(SSD_INNER), vec(RET_V_DIM),
            pl.BlockSpec((MIX_WIDTH, d), lambda b, i: (0, 0)),
            pl.BlockSpec((None, 1, d), lambda b, i: (b, 0, 0)),
            vec(d), vec(d),
        ],
        out_specs=row(d, 0),
        out_shape=jax.ShapeDtypeStruct((bsz, t, d), f32),
        compiler_params=_cparams(("parallel", "parallel")),
        name="mix_out",
    )(x, y_ssd, y_ssd, xbc, proj, y_diff, y_ret, y_ret, proj,
      d_skip, ssd_norm_w.reshape(1, -1), ret_norm_w.reshape(1, -1), w_out, g_a, ln_w.reshape(1, -1), ln_b.reshape(1, -1))


def _gelu_exact(x):
    return 0.5 * x * (1.0 + lax.erf(x * (2.0 ** -0.5)))


FFN_SPLIT = 1


def _ffn_kernel(x_ref, xp_ref, xn_ref, sc_ref, sh_ref, gf_ref, wu_ref, wv_ref, cw_ref, cb_ref, wd_ref,
                lw_ref, lb_ref, o_ref, xe_ref, ue_ref, acc_ref, *, tm, nt, nf, alpha):
    i = pl.program_id(1)
    c = pl.program_id(2)
    h = SUBLANES_V7X
    hm = tm // FFN_SPLIT

    @pl.when(c == 0)
    def _():
        scale = 1.0 + sc_ref[...]
        shift = sh_ref[...]
        xe_ref[0:h, :] = jnp.where(i > 0, xp_ref[...] * scale + shift, 0.0)
        xe_ref[h:h + tm, :] = x_ref[...] * scale + shift
        xe_ref[h + tm:2 * h + tm, :] = jnp.where(i < nt - 1, xn_ref[...] * scale + shift, 0.0)
        acc_ref[...] = jnp.zeros(acc_ref.shape, f32)

    vs = []
    for r in range(FFN_SPLIT):
        r0 = r * hm
        ue_ref[r] = jnp.dot(xe_ref[r0:r0 + hm + 2 * h, :].astype(bf16), wu_ref[...], preferred_element_type=f32)
        vs.append(jnp.dot(xe_ref[h + r0:h + r0 + hm, :].astype(bf16), wv_ref[...], preferred_element_type=f32))
    p = FFN_CONV // 2
    for r in range(FFN_SPLIT):
        ue = ue_ref[r]
        n = ue.shape[0]
        u = cb_ref[...] + ue[h:h + hm] * cw_ref[p:p + 1, :]
        for j in range(FFN_CONV):
            if j != p:
                u = u + pltpu.roll(ue, (p - j) % n, 0)[h:h + hm] * cw_ref[j:j + 1, :]
        gated = (_gelu_exact(u) * vs[r]).astype(bf16)
        acc_ref[r * hm:(r + 1) * hm, :] += jnp.dot(gated, wd_ref[...], preferred_element_type=f32)

    @pl.when(c == nf - 1)
    def _():
        o_ref[...] = _layernorm(alpha * x_ref[...] + gf_ref[...] * acc_ref[...], lw_ref[...], lb_ref[...])


def _ffn_call(x, sc, sh, gf, w_up, conv_w, conv_b, w_down, ln_w, ln_b, alpha):
    bsz, t, d = x.shape
    tm = min(512, t)
    nt = t // tm
    tf = D_FF
    nf = D_FF // tf
    hb = tm // SUBLANES_V7X
    last_hb = t // SUBLANES_V7X - 1
    mod = pl.BlockSpec((None, 1, d), lambda b, i, c: (b, 0, 0))
    vec = pl.BlockSpec((1, d), lambda b, i, c: (0, 0))
    resident = dict(pipeline_mode=pl.Buffered(1)) if nf == 1 else {}
    return pl.pallas_call(
        functools.partial(_ffn_kernel, tm=tm, nt=nt, nf=nf, alpha=alpha),
        grid=(bsz, nt, nf),
        in_specs=[
            pl.BlockSpec((None, tm, d), lambda b, i, c: (b, i, 0)),
            pl.BlockSpec((None, SUBLANES_V7X, d), lambda b, i, c: (b, jnp.maximum(i * hb - 1, 0), 0)),
            pl.BlockSpec((None, SUBLANES_V7X, d), lambda b, i, c: (b, jnp.minimum((i + 1) * hb, last_hb), 0)),
            mod, mod, mod,
            pl.BlockSpec((d, tf), lambda b, i, c: (0, c), **resident),
            pl.BlockSpec((d, tf), lambda b, i, c: (0, nf + c), **resident),
            pl.BlockSpec((FFN_CONV, tf), lambda b, i, c: (0, c)),
            pl.BlockSpec((1, tf), lambda b, i, c: (0, c)),
            pl.BlockSpec((tf, d), lambda b, i, c: (c, 0), **resident),
            vec, vec,
        ],
        out_specs=pl.BlockSpec((None, tm, d), lambda b, i, c: (b, i, 0)),
        out_shape=jax.ShapeDtypeStruct((bsz, t, d), f32),
        scratch_shapes=[
            pltpu.VMEM((tm + 2 * SUBLANES_V7X, d), f32),
            pltpu.VMEM((FFN_SPLIT, tm // FFN_SPLIT + 2 * SUBLANES_V7X, tf), f32),
            pltpu.VMEM((tm, d), f32),
        ],
        compiler_params=_cparams(("parallel", "parallel", "arbitrary")),
        name="conv_ffn",
    )(x, x, x, sc, sh, gf, w_up, w_up, conv_w, conv_b.reshape(1, -1), w_down, ln_w.reshape(1, -1), ln_b.reshape(1, -1))


def _ext_weight(w_in):
    d = w_in.shape[0]
    o = 0
    z = w_in[:, o:o + SSD_INNER]; o += SSD_INNER
    xbc = w_in[:, o:o + SSD_XBC]; o += SSD_XBC
    dt = w_in[:, o:o + 2 * SSD_HEADS]; o += 2 * SSD_HEADS
    dq = w_in[:, o:o + DIFF_QK]; o += DIFF_QK
    dk = w_in[:, o:o + DIFF_QK]; o += DIFF_QK
    dv = w_in[:, o:o + DIFF_WIDTH]; o += DIFF_WIDTH
    rq = w_in[:, o:o + RET_QK]; o += RET_QK
    rk = w_in[:, o:o + RET_QK]; o += RET_QK
    rv = w_in[:, o:o + RET_WIDTH]; o += RET_WIDTH
    rg = w_in[:, o:o + RET_WIDTH]; o += RET_WIDTH
    dt_pad = jnp.zeros((d, LANES_V7X - SSD_HEADS), w_in.dtype)
    main = jnp.concatenate([z, dq, dk, dv, rq, rk, rv, rg], axis=1)
    w_dt = jnp.concatenate([dt[:, :SSD_HEADS], dt_pad, dt[:, SSD_HEADS:], dt_pad], axis=1)
    return main.astype(bf16), xbc.astype(bf16), w_dt.astype(bf16)


def _rope_tables(s, n_ctx):
    rows = s // GRID_W
    row = jnp.repeat(jnp.arange(rows, dtype=f32), GRID_W)
    col = jnp.tile(jnp.arange(GRID_W, dtype=f32), rows)
    n_ax = DIFF_HEAD_DIM // 4
    inv_ax = 1.0 / (ROPE_BASE ** (jnp.arange(n_ax, dtype=f32) / n_ax))
    ar = row[:, None] * inv_ax
    ac = col[:, None] * inv_ax
    ang = jnp.concatenate([ar, ar, ac, ac], axis=-1)
    ang = jnp.tile(ang, (1, 2))
    inv_ret = 1.0 / (ROPE_BASE ** jnp.linspace(0.0, 1.0, RET_QK_DIM // 2, dtype=f32))
    ang_c = jnp.arange(n_ctx, dtype=f32)[:, None] * inv_ret
    ang_l = (n_ctx + jnp.arange(s, dtype=f32))[:, None] * inv_ret
    tile_ret = lambda a: jnp.tile(jnp.concatenate([a, a], axis=-1), (1, RET_HEADS))
    ang = jnp.tile(ang, (1, DIFF_HEADS))
    return {
        "diff_l": (jnp.cos(ang), jnp.sin(ang)),
        "diff_c": (jnp.ones((n_ctx, DIFF_WIDTH), f32), jnp.zeros((n_ctx, DIFF_WIDTH), f32)),
        "ret_l": (jnp.cos(tile_ret(ang_l)), jnp.sin(tile_ret(ang_l))),
        "ret_c": (jnp.cos(tile_ret(ang_c)), jnp.sin(tile_ret(ang_c))),
    }


def kernel(x, c, ctx, c_ctx, w_ada, b_ada, w_in, ssd_conv_w, ssd_conv_b, ssd_a_log, ssd_dt_bias, ssd_d, ssd_norm_w,
           diff_lambda, diff_norm_w, ret_decay, ret_norm_w, w_out, ln1_w, ln1_b, ffn_w_up, ffn_conv_w, ffn_conv_b,
           ffn_w_down, ln2_w, ln2_b):
    bsz, s, d = x.shape
    n_ctx = ctx.shape[1]
    depth = w_ada.shape[0]
    alpha = (2.0 * depth) ** 0.25
    assert d == D_MODEL and s % 512 == 0 and n_ctx % ATT_TP == 0 and s % GRID_W == 0

    tabs = _rope_tables(s, n_ctx)
    nrow = -(-(bsz + 1) // SUBLANES_V7X) * SUBLANES_V7X
    cvecs = jnp.zeros((nrow, d), f32).at[:bsz].set(c).at[bsz].set(c_ctx)
    mod_all = _ada_call(cvecs, w_ada, b_ada)

    gw = SSD_HPG * SSD_HEAD_DIM
    zero_ssd = jnp.zeros((bsz, 2, SSD_GROUPS, SSD_STATE, gw), f32)
    zero_ret = jnp.zeros((bsz, 2, RET_HEADS // 2, 2 * RET_QK_DIM, RET_V_DIM), f32)

    xc = ctx
    for li in range(depth):
        last = li == depth - 1
        lam_init = 0.8 - 0.6 * math.exp(-0.3 * li)
        mod = mod_all[li]
        m_lat = [mod[:bsz, k * d:(k + 1) * d].reshape(bsz, 1, d) for k in range(6)]
        m_ctx = [jnp.broadcast_to(mod[bsz, k * d:(k + 1) * d].reshape(1, 1, d), (bsz, 1, d)) for k in range(6)]
        w_main, w_xbc, w_dt = _ext_weight(w_in[li])
        w_out_b = w_out[li].astype(bf16)
        w_up_b = ffn_w_up[li].astype(bf16)
        w_down_b = ffn_w_down[li].astype(bf16)
        d_skip = jnp.repeat(ssd_d[li], SSD_HEAD_DIM).reshape(1, SSD_INNER)

        proj, xbc, dt_l = _in_proj_call(x, m_lat[1], m_lat[0], w_main, w_xbc, w_dt,
                                        ssd_conv_w[li], ssd_conv_b[li])
        proj_c, xbc_c, dt_c = _in_proj_call(xc, m_ctx[1], m_ctx[0], w_main, w_xbc, w_dt,
                                            ssd_conv_w[li], ssd_conv_b[li])

        ys_c, hs = _ssd_scan_call(xbc_c, dt_c, ssd_a_log[li], ssd_dt_bias[li], zero_ssd)
        ys, _ = _ssd_scan_call(xbc, dt_l, ssd_a_log[li], ssd_dt_bias[li], hs)

        yr_c, hr = _ret_scan_call(proj_c, *tabs["ret_c"], ret_decay[li], zero_ret)
        yr, _ = _ret_scan_call(proj, *tabs["ret_l"], ret_decay[li], hr)

        qt_c, k_c, vt_c = _attn_prep_call(proj_c, *tabs["diff_c"])
        qt_l, k_l, vt_l = _attn_prep_call(proj, *tabs["diff_l"])
        y_diff = _attn_call(qt_l, (k_l, vt_l), (k_c, vt_c), diff_lambda[li], diff_norm_w[li], lam_init)

        x = _mix_out_call(x, ys, xbc, proj, y_diff, yr, d_skip, ssd_norm_w[li], ret_norm_w[li], w_out_b,
                          m_lat[2], ln1_w[li], ln1_b[li], alpha)
        x = _ffn_call(x, m_lat[4], m_lat[3], m_lat[5], w_up_b, ffn_conv_w[li], ffn_conv_b[li], w_down_b,
                      ln2_w[li], ln2_b[li], alpha)
        if not last:
            yc_diff = _attn_call(qt_c, None, (k_c, vt_c), diff_lambda[li], diff_norm_w[li], lam_init)
            xc = _mix_out_call(xc, ys_c, xbc_c, proj_c, yc_diff, yr_c, d_skip, ssd_norm_w[li], ret_norm_w[li], w_out_b,
                               m_ctx[2], ln1_w[li], ln1_b[li], alpha)
            xc = _ffn_call(xc, m_ctx[4], m_ctx[3], m_ctx[5], w_up_b, ffn_conv_w[li], ffn_conv_b[li], w_down_b,
                           ln2_w[li], ln2_b[li], alpha)
    return x
```
